```python
import math
import jax, jax.numpy as jnp
from jax import lax
import numpy as np

D_MODEL = 1024
BATCH = 8
SEQ = 4096
DEPTH = 1

N_HEADS = 8
N_KV = 2
HEAD_DIM = 64
HPG = N_HEADS // N_KV
NSA_WIDTH = N_HEADS * HEAD_DIM
KV_WIDTH = N_KV * HEAD_DIM
ROPE_DIM = HEAD_DIM // 4
ROPE_THETA = 500000.0
CMP_BLOCK = 32
CMP_STRIDE = 16
SLC_BLOCK = 64
N_SLC = 16
WINDOW = 512
Q_BLOCK = 128
PHI_HIDDEN = 256
N_BRANCH = 3
LRU_WIDTH = D_MODEL
LRU_BLOCKS = 8
LRU_BD = LRU_WIDTH // LRU_BLOCKS
LRU_C = 8.0
CONV_W = 4
D_FF = 4 * D_MODEL
EPS = 1e-6
NEG = -1e30
FORCE_SCORE = 1e4
COL_WIDTHS = (NSA_WIDTH,) + (KV_WIDTH,) * 6 + (N_BRANCH * N_HEADS, LRU_WIDTH, LRU_WIDTH, 2 * D_MODEL)
IN_COLS = sum(COL_WIDTHS)

kernel_name = "hybrid_nsa_rglru_sqrelu_block"


def rmsnorm(x, w):
    xf = x.astype(jnp.float32)
    y = xf * lax.rsqrt(jnp.mean(xf * xf, axis=-1, keepdims=True) + EPS)
    return (y * w.astype(jnp.float32)).astype(x.dtype)


def rope_partial(x, pos):
    half = ROPE_DIM // 2
    inv = ROPE_THETA ** (-jnp.arange(half, dtype=jnp.float32) / half)
    ang = pos.astype(jnp.float32)[:, None] * inv[None, :]
    cos, sin = jnp.cos(ang), jnp.sin(ang)
    xr = x[..., :ROPE_DIM].astype(jnp.float32)
    x1, x2 = xr[..., :half], xr[..., half:]
    rot = jnp.concatenate([x1 * cos - x2 * sin, x2 * cos + x1 * sin], axis=-1).astype(x.dtype)
    return jnp.concatenate([rot, x[..., ROPE_DIM:]], axis=-1)


def kv_heads(t):
    b, s, _ = t.shape
    return t.reshape(b, s, N_KV, HEAD_DIM).transpose(0, 2, 1, 3)


def compress_mlp(blocks, pos_emb, w1, w2):
    z = (blocks + pos_emb).reshape(blocks.shape[:3] + (CMP_BLOCK * HEAD_DIM,))
    return jax.nn.gelu(z @ w1) @ w2


def nsa_mixer(q_raw, kc_raw, vc_raw, ks_raw, vs_raw, kw_raw, vw_raw, g_raw,
              q_norm_w, k_norm_w, phi_k_pos, phi_k_w1, phi_k_w2, phi_v_pos, phi_v_w1, phi_v_w2):
    B, S, _ = q_raw.shape
    dt = q_raw.dtype
    scale = HEAD_DIM ** -0.5
    t = jnp.arange(S, dtype=jnp.int32)
    nqb = S // Q_BLOCK

    q = q_raw.reshape(B, S, N_KV, HPG, HEAD_DIM).transpose(0, 2, 3, 1, 4)
    q = rope_partial(rmsnorm(q, q_norm_w), t)

    n_cmp = (S - CMP_BLOCK) // CMP_STRIDE + 1
    cmp_idx = jnp.arange(n_cmp)[:, None] * CMP_STRIDE + jnp.arange(CMP_BLOCK)[None, :]
    cmp_end = cmp_idx[:, -1]
    kc = compress_mlp(kv_heads(kc_raw)[:, :, cmp_idx], phi_k_pos, phi_k_w1, phi_k_w2)
    vc = compress_mlp(kv_heads(vc_raw)[:, :, cmp_idx], phi_v_pos, phi_v_w1, phi_v_w2)
    kc = rope_partial(rmsnorm(kc, k_norm_w[0]), cmp_end)
    s_c = jnp.einsum('bghsd,bgnd->bghsn', q, kc).astype(jnp.float32) * scale
    mask_c = cmp_end[None, :] <= t[:, None]
    p_c = jax.nn.softmax(jnp.where(mask_c, s_c, NEG), axis=-1)
    p_c = jnp.where((t >= CMP_BLOCK - 1)[:, None], p_c, 0.0)
    o_cmp = jnp.einsum('bghsn,bgnd->bghsd', p_c.astype(dt), vc)

    n_slc = S // SLC_BLOCK
    n_sel = min(N_SLC, n_slc)
    blk = jnp.arange(n_slc)
    slc_lo = blk * SLC_BLOCK
    overlap = ((cmp_idx[:, 0][:, None] <= (slc_lo + SLC_BLOCK - 1)[None, :]) &
               (cmp_end[:, None] >= slc_lo[None, :])).astype(jnp.float32)
    imp = jnp.einsum('bghsn,nj->bgsj', p_c, overlap)
    cur = t // SLC_BLOCK
    forced = (blk[None, :] == 0) | (blk[None, :] == cur[:, None]) | (blk[None, :] == cur[:, None] - 1)
    causal_ok = blk[None, :] <= cur[:, None]
    score = jnp.where(causal_ok, jnp.where(forced, FORCE_SCORE, imp), NEG)
    _, sel = lax.top_k(score, n_sel)

    k_s = rope_partial(rmsnorm(kv_heads(ks_raw), k_norm_w[1]), t)
    v_s = kv_heads(vs_raw)
    b_ix = jnp.arange(B)[:, None, None]
    g_ix = jnp.arange(N_KV)[None, :, None]
    q_blk = q.reshape(B, N_KV, HPG, nqb, Q_BLOCK, HEAD_DIM).transpose(3, 0, 1, 2, 4, 5)
    sel_blk = sel.reshape(B, N_KV, nqb, Q_BLOCK, n_sel).transpose(2, 0, 1, 3, 4)
    t_blk = t.reshape(nqb, Q_BLOCK)

    def attend_selected(args):
        qb, sb, tb = args
        tok = sb[..., None] * SLC_BLOCK + jnp.arange(SLC_BLOCK)
        tok = tok.reshape(B, N_KV, Q_BLOCK * n_sel * SLC_BLOCK)
        kg = k_s[b_ix, g_ix, tok].reshape(B, N_KV, Q_BLOCK, n_sel * SLC_BLOCK, HEAD_DIM)
        vg = v_s[b_ix, g_ix, tok].reshape(B, N_KV, Q_BLOCK, n_sel * SLC_BLOCK, HEAD_DIM)
        s = jnp.einsum('bghqd,bgqkd->bghqk', qb, kg).astype(jnp.float32) * scale
        m = tok.reshape(B, N_KV, Q_BLOCK, n_sel * SLC_BLOCK) <= tb[None, None, :, None]
        p = jax.nn.softmax(jnp.where(m[:, :, None], s, NEG), axis=-1)
        return jnp.einsum('bghqk,bgqkd->bghqd', p.astype(dt), vg)

    o_slc = lax.map(attend_selected, (q_blk, sel_blk, t_blk))
    o_slc = o_slc.transpose(1, 2, 3, 0, 4, 5).reshape(B, N_KV, HPG, S, HEAD_DIM)

    k_w = rope_partial(rmsnorm(kv_heads(kw_raw), k_norm_w[2]), t)
    v_w = kv_heads(vw_raw)
    win_pos = jnp.arange(nqb)[:, None] * Q_BLOCK + jnp.arange(WINDOW + Q_BLOCK)[None, :] - WINDOW
    safe = jnp.clip(win_pos, 0, S - 1)
    kb = k_w[:, :, safe]
    vb = v_w[:, :, safe]
    qw = q.reshape(B, N_KV, HPG, nqb, Q_BLOCK, HEAD_DIM)
    s_w = jnp.einsum('bghnqd,bgnkd->bghnqk', qw, kb).astype(jnp.float32) * scale
    kp = win_pos[:, None, :]
    tq = t_blk[:, :, None]
    mask_w = (kp <= tq) & (kp > tq - WINDOW) & (kp >= 0)
    p_w = jax.nn.softmax(jnp.where(mask_w, s_w, NEG), axis=-1)
    o_win = jnp.einsum('bghnqk,bgnkd->bghnqd', p_w.astype(dt), vb).reshape(B, N_KV, HPG, S, HEAD_DIM)

    g = jax.nn.sigmoid(g_raw).reshape(B, S, N_BRANCH, N_KV, HPG).transpose(2, 0, 3, 4, 1)[..., None]
    o = g[0] * o_cmp + g[1] * o_slc + g[2] * o_win
    return o.transpose(0, 3, 1, 2, 4).reshape(B, S, NSA_WIDTH)


def rglru_mixer(xr, yr, conv_w, conv_b, lru_wa, lru_ba, lru_wx, lru_bx, lru_lambda):
    B, S, W = xr.shape
    xp = jnp.pad(xr, ((0, 0), (CONV_W - 1, 0), (0, 0)))
    xc = conv_b + sum(xp[:, j:j + S] * conv_w[j] for j in range(CONV_W))
    xb = xc.reshape(B, S, LRU_BLOCKS, LRU_BD)
    r = jax.nn.sigmoid(jnp.einsum('bsnd,nde->bsne', xb, lru_wa) + lru_ba).reshape(B, S, W)
    i_g = jax.nn.sigmoid(jnp.einsum('bsnd,nde->bsne', xb, lru_wx) + lru_bx).reshape(B, S, W)
    log_a = -LRU_C * jax.nn.softplus(-lru_lambda.astype(jnp.float32)) * r.astype(jnp.float32)
    a = jnp.exp(log_a)
    bterm = jnp.sqrt(-jnp.expm1(2.0 * log_a)) * (i_g.astype(jnp.float32) * xc.astype(jnp.float32))

    def combine(left, right):
        a1, b1 = left
        a2, b2 = right
        return a1 * a2, a2 * b1 + b2

    _, h = lax.associative_scan(combine, (a, bterm), axis=1)
    return h.astype(xr.dtype) * jax.nn.gelu(yr)


def setup_inputs(seed: int = 0) -> dict:
    key = jax.random.key(seed)
    ks = jax.random.split(key, 24)
    L = DEPTH

    def nrm(k, shape, scale):
        return jax.random.normal(k, shape, jnp.float32) * scale

    u = jax.random.uniform(ks[17], (L, LRU_WIDTH), jnp.float32, 0.9, 0.999)
    a0 = u ** (1.0 / LRU_C)
    return {
        "x": nrm(ks[0], (BATCH, SEQ, D_MODEL), 1.0),
        "norm1_w": 1.0 + nrm(ks[1], (L, D_MODEL), 0.02),
        "w_in": nrm(ks[2], (L, D_MODEL, IN_COLS), D_MODEL ** -0.5),
        "q_norm_w": 1.0 + nrm(ks[3], (L, HEAD_DIM), 0.02),
        "k_norm_w": 1.0 + nrm(ks[4], (L, N_BRANCH, HEAD_DIM), 0.02),
        "phi_k_pos": nrm(ks[5], (L, CMP_BLOCK, HEAD_DIM), 0.02),
        "phi_k_w1": nrm(ks[6], (L, CMP_BLOCK * HEAD_DIM, PHI_HIDDEN), (CMP_BLOCK * HEAD_DIM) ** -0.5),
        "phi_k_w2": nrm(ks[7], (L, PHI_HIDDEN, HEAD_DIM), PHI_HIDDEN ** -0.5),
        "phi_v_pos": nrm(ks[8], (L, CMP_BLOCK, HEAD_DIM), 0.02),
        "phi_v_w1": nrm(ks[9], (L, CMP_BLOCK * HEAD_DIM, PHI_HIDDEN), (CMP_BLOCK * HEAD_DIM) ** -0.5),
        "phi_v_w2": nrm(ks[10], (L, PHI_HIDDEN, HEAD_DIM), PHI_HIDDEN ** -0.5),
        "conv_w": nrm(ks[11], (L, CONV_W, LRU_WIDTH), CONV_W ** -0.5),
        "conv_b": nrm(ks[12], (L, LRU_WIDTH), 0.01),
        "lru_wa": nrm(ks[13], (L, LRU_BLOCKS, LRU_BD, LRU_BD), LRU_BD ** -0.5),
        "lru_ba": nrm(ks[14], (L, LRU_BLOCKS, LRU_BD), 0.01),
        "lru_wx": nrm(ks[15], (L, LRU_BLOCKS, LRU_BD, LRU_BD), LRU_BD ** -0.5),
        "lru_bx": nrm(ks[16], (L, LRU_BLOCKS, LRU_BD), 0.01),
        "lru_lambda": jnp.log(a0) - jnp.log1p(-a0),
        "w_nsa_up": nrm(ks[18], (L, NSA_WIDTH, D_MODEL), NSA_WIDTH ** -0.5),
        "w_lru_up": nrm(ks[19], (L, LRU_WIDTH, D_MODEL), LRU_WIDTH ** -0.5),
        "w_o": nrm(ks[20], (L, D_MODEL, D_MODEL), D_MODEL ** -0.5),
        "norm2_w": 1.0 + nrm(ks[21], (L, D_MODEL), 0.02),
        "w_ff1": nrm(ks[22], (L, D_MODEL, D_FF), D_MODEL ** -0.5),
        "w_ff2": nrm(ks[23], (L, D_FF, D_MODEL), D_FF ** -0.5),
    }


def reference(x, norm1_w, w_in, q_norm_w, k_norm_w, phi_k_pos, phi_k_w1, phi_k_w2,
              phi_v_pos, phi_v_w1, phi_v_w2, conv_w, conv_b, lru_wa, lru_ba, lru_wx, lru_bx,
              lru_lambda, w_nsa_up, w_lru_up, w_o, norm2_w, w_ff1, w_ff2):
    h = x
    for l in range(DEPTH):
        xn = rmsnorm(h, norm1_w[l])
        proj = xn @ w_in[l]
        parts = []
        off = 0
        for wdt in COL_WIDTHS:
            parts.append(proj[..., off:off + wdt])
            off += wdt
        q_raw, kc_raw, vc_raw, ks_raw, vs_raw, kw_raw, vw_raw, g_nsa, xr, yr, g_merge = parts

        o_nsa = nsa_mixer(q_raw, kc_raw, vc_raw, ks_raw, vs_raw, kw_raw, vw_raw, g_nsa,
                          q_norm_w[l], k_norm_w[l], phi_k_pos[l], phi_k_w1[l], phi_k_w2[l],
                          phi_v_pos[l], phi_v_w1[l], phi_v_w2[l])
        o_lru = rglru_mixer(xr, yr, conv_w[l], conv_b[l], lru_wa[l], lru_ba[l], lru_wx[l],
                            lru_bx[l], lru_lambda[l])
        gate = jax.nn.sigmoid(g_merge)
        merged = gate[..., :D_MODEL] * (o_nsa @ w_nsa_up[l]) + gate[..., D_MODEL:] * (o_lru @ w_lru_up[l])
        h = h + merged @ w_o[l]

        hn = rmsnorm(h, norm2_w[l])
        h = h + jnp.square(jax.nn.relu(hn @ w_ff1[l])) @ w_ff2[l]
    return h
```

```python
import functools

import jax
import jax.numpy as jnp
from jax import lax
from jax.experimental import pallas as pl
from jax.experimental.pallas import tpu as pltpu

F32 = jnp.float32
BF16 = jnp.bfloat16

D_MODEL = 1024
N_HEADS = 8
N_KV = 2
HEAD_DIM = 64
HPG = N_HEADS // N_KV
NSA_WIDTH = N_HEADS * HEAD_DIM
KV_WIDTH = N_KV * HEAD_DIM
ROPE_DIM = HEAD_DIM // 4
ROPE_HALF = ROPE_DIM // 2
ROPE_THETA = 500000.0
CMP_BLOCK = 32
CMP_STRIDE = 16
SLC_BLOCK = 64
N_SLC = 16
WINDOW = 512
Q_BLOCK = 128
PHI_HIDDEN = 256
N_BRANCH = 3
LRU_WIDTH = D_MODEL
LRU_BLOCKS = 8
LRU_BD = LRU_WIDTH // LRU_BLOCKS
LRU_C = 8.0
CONV_W = 4
D_FF = 4 * D_MODEL
EPS = 1e-6
NEG = -1e30
FORCE_SCORE = 1e4

LANES = 128
SUBLANES = 8
VMEM_LIMIT_BYTES = 56 * 1024 * 1024

G_PAD = LANES
OFF_Q = 0
OFF_KC = OFF_Q + NSA_WIDTH
OFF_VC = OFF_KC + KV_WIDTH
OFF_KS = OFF_VC + KV_WIDTH
OFF_VS = OFF_KS + KV_WIDTH
OFF_KW = OFF_VS + KV_WIDTH
OFF_VW = OFF_KW + KV_WIDTH
OFF_G = OFF_VW + KV_WIDTH
OFF_XR = OFF_G + G_PAD
OFF_YR = OFF_XR + LRU_WIDTH
OFF_GM = OFF_YR + LRU_WIDTH
IN_COLS_PAD = OFF_GM + 2 * D_MODEL
G_REAL_END = NSA_WIDTH + 6 * KV_WIDTH + N_BRANCH * N_HEADS

TM_IN = 256
TS_LRU = 256
TM_OUT = 256
TK_SLC = 512
TK_WIN = 128
FF_CHUNK = 1024


def _const_spec(shape):
    nd = len(shape)
    return pl.BlockSpec(shape, lambda *_: (0,) * nd)


def _head_rmsnorm(v, w_tiled, bd):
    sq = v * v
    hi = sq.astype(BF16)
    lo = (sq - hi.astype(F32)).astype(BF16)
    ss = jnp.dot(hi, bd, preferred_element_type=F32) + jnp.dot(lo, bd, preferred_element_type=F32)
    return v * lax.rsqrt(ss * (1.0 / HEAD_DIM) + EPS) * w_tiled


def _rope(v, c, sa, sb):
    w = v.shape[-1]
    reps = w // LANES
    if reps > 1:
        c = jnp.concatenate([c] * reps, axis=1)
        sa = jnp.concatenate([sa] * reps, axis=1)
        sb = jnp.concatenate([sb] * reps, axis=1)
    up = pltpu.roll(v, w - ROPE_HALF, 1)
    dn = pltpu.roll(v, ROPE_HALF, 1)
    return v * c + up * sa + dn * sb


def _inproj_kernel(x_ref, n1_ref, w_ref, qnw_ref, ksw_ref, kww_ref, rc_ref, ra_ref, rb_ref, bd_ref,
                   q_ref, kcv_ref, ks_ref, vs_ref, kw_ref, vw_ref, g_ref, xr_ref, gy_ref, gm_ref):
    x = x_ref[0]
    ms = jnp.mean(x * x, axis=-1, keepdims=True)
    xn = (x * lax.rsqrt(ms + EPS) * n1_ref[...]).astype(BF16)

    def proj(off, width):
        return jnp.dot(xn, w_ref[:, off:off + width], preferred_element_type=F32)

    rc, ra, rb = rc_ref[...], ra_ref[...], rb_ref[...]
    bd = bd_ref[...]
    bd_kv = bd[:KV_WIDTH, :KV_WIDTH]

    q = _rope(_head_rmsnorm(proj(OFF_Q, NSA_WIDTH), qnw_ref[...], bd), rc, ra, rb)
    q_ref[0] = (q * (HEAD_DIM ** -0.5)).astype(BF16)
    kcv_ref[0] = proj(OFF_KC, 2 * KV_WIDTH).astype(BF16)
    ks = _rope(_head_rmsnorm(proj(OFF_KS, KV_WIDTH), ksw_ref[...], bd_kv), rc, ra, rb)
    ks_ref[0] = ks.astype(BF16)
    vs_ref[0] = proj(OFF_VS, KV_WIDTH).astype(BF16)
    kw = _rope(_head_rmsnorm(proj(OFF_KW, KV_WIDTH), kww_ref[...], bd_kv), rc, ra, rb)
    kw_ref[0] = kw.astype(BF16)
    vw_ref[0] = proj(OFF_VW, KV_WIDTH).astype(BF16)
    g_ref[0] = jax.nn.sigmoid(proj(OFF_G, G_PAD))
    xr_ref[0] = proj(OFF_XR, LRU_WIDTH).astype(BF16)
    gy_ref[0] = jax.nn.gelu(proj(OFF_YR, LRU_WIDTH)).astype(BF16)
    gm_ref[0] = jax.nn.sigmoid(proj(OFF_GM, 2 * D_MODEL)).astype(BF16)


def _in_proj(x, n1, w_pad, qnw, ksw, kww, rc, ra, rb, bd):
    B, S, D = x.shape
    tm = TM_IN
    tok = lambda width: pl.BlockSpec((1, tm, width), lambda b, s: (b, s, 0))
    tab = pl.BlockSpec((tm, LANES), lambda b, s: (s, 0))
    out_widths = [NSA_WIDTH, 2 * KV_WIDTH, KV_WIDTH, KV_WIDTH, KV_WIDTH, KV_WIDTH, G_PAD,
                  LRU_WIDTH, LRU_WIDTH, 2 * D_MODEL]
    out_dtypes = [BF16, BF16, BF16, BF16, BF16, BF16, F32, BF16, BF16, BF16]
    return pl.pallas_call(
        _inproj_kernel,
        grid=(B, S // tm),
        in_specs=[tok(D), _const_spec((1, D)), _const_spec((D, IN_COLS_PAD)),
                  _const_spec((1, NSA_WIDTH)), _const_spec((1, KV_WIDTH)), _const_spec((1, KV_WIDTH)),
                  tab, tab, tab, _const_spec((NSA_WIDTH, NSA_WIDTH))],
        out_specs=[tok(w) for w in out_widths],
        out_shape=[jax.ShapeDtypeStruct((B, S, w), dt) for w, dt in zip(out_widths, out_dtypes)],
        compiler_params=pltpu.CompilerParams(
            dimension_semantics=("parallel", "parallel"), vmem_limit_bytes=VMEM_LIMIT_BYTES),
        name="in_proj",
    )(x, n1, w_pad, qnw, ksw, kww, rc, ra, rb, bd)


def _compress_kernel(xk_ref, xv_ref, pk_ref, pv_ref, w1k_ref, w1v_ref, w2k_ref, w2v_ref,
                     knw_ref, rc_ref, ra_ref, rb_ref, kc_ref, vc_ref):
    half = CMP_STRIDE * HEAD_DIM

    def mlp(x_ref, p_ref, w1_ref, w2_ref):
        x = x_ref[0, 0].astype(F32)
        lo = (x + p_ref[0:1, :]).astype(BF16)
        hi = (x + p_ref[1:2, :]).astype(BF16)
        a = jnp.dot(lo, w1_ref[:half, :], preferred_element_type=F32)
        b = jnp.dot(hi, w1_ref[half:, :], preferred_element_type=F32)
        hid = a + pltpu.roll(b, b.shape[0] - 1, 0)
        return jnp.dot(jax.nn.gelu(hid).astype(BF16), w2_ref[...], preferred_element_type=F32)

    kc = mlp(xk_ref, pk_ref, w1k_ref, w2k_ref)
    ms = jnp.sum(kc * kc, axis=-1, keepdims=True) * (1.0 / HEAD_DIM)
    kc = kc * lax.rsqrt(ms + EPS) * knw_ref[...]
    kc_ref[0, 0] = _rope(kc, rc_ref[...], ra_ref[...], rb_ref[...]).astype(BF16)
    vc_ref[0, 0] = mlp(xv_ref, pv_ref, w1v_ref, w2v_ref).astype(BF16)


def _compress(xk, xv, pk, pv, w1k, w1v, w2k, w2v, knw, rc, ra, rb):
    B, G, C, W = xk.shape
    blk = pl.BlockSpec((1, 1, C, W), lambda b, g: (b, g, 0, 0))
    oblk = pl.BlockSpec((1, 1, C, LANES), lambda b, g: (b, g, 0, 0))
    return pl.pallas_call(
        _compress_kernel,
        grid=(B, G),
        in_specs=[blk, blk, _const_spec(pk.shape), _const_spec(pv.shape),
                  _const_spec(w1k.shape), _const_spec(w1v.shape), _const_spec(w2k.shape), _const_spec(w2v.shape),
                  _const_spec(knw.shape), _const_spec(rc.shape), _const_spec(ra.shape), _const_spec(rb.shape)],
        out_specs=[oblk, oblk],
        out_shape=[jax.ShapeDtypeStruct((B, G, C, LANES), BF16)] * 2,
        compiler_params=pltpu.CompilerParams(
            dimension_semantics=("parallel", "parallel"), vmem_limit_bytes=VMEM_LIMIT_BYTES),
        name="compress",
    )(xk, xv, pk, pv, w1k, w1v, w2k, w2v, knw, rc, ra, rb)


def _rglru_kernel(xr_ref, gy_ref, cw_ref, cb_ref, wax_ref, bax_ref, lam_ref, o_ref, tail_ref, h_ref):
    ts = xr_ref.shape[1]

    @pl.when(pl.program_id(1) == 0)
    def _():
        tail_ref[...] = jnp.zeros_like(tail_ref)
        h_ref[...] = jnp.zeros_like(h_ref)

    x = xr_ref[0].astype(F32)
    xe = jnp.concatenate([tail_ref[...], x], axis=0)
    tail_ref[...] = x[ts - SUBLANES:, :]
    xc = cb_ref[...] + x * cw_ref[CONV_W - 1:CONV_W, :]
    for j in range(CONV_W - 1):
        back = CONV_W - 1 - j
        xc = xc + xe[SUBLANES - back:SUBLANES - back + ts, :] * cw_ref[j:j + 1, :]

    xcb = xc.astype(BF16)
    pre = [jnp.dot(xcb[:, n * LRU_BD:(n + 1) * LRU_BD], wax_ref[n], preferred_element_type=F32)
           for n in range(LRU_BLOCKS)]
    r = jax.nn.sigmoid(jnp.concatenate([p[:, :LRU_BD] for p in pre], axis=1) + bax_ref[0:1, :])
    ig = jax.nn.sigmoid(jnp.concatenate([p[:, LRU_BD:] for p in pre], axis=1) + bax_ref[1:2, :])

    lam = lam_ref[...]
    log_a = (-LRU_C * jax.nn.softplus(-lam)) * r
    a = jnp.exp(log_a)
    b = jnp.sqrt(-jnp.tanh(log_a) * (1.0 + a * a)) * (ig * xc)

    row = lax.broadcasted_iota(jnp.int32, (ts, 1), 0)
    d = 1
    while d < ts:
        a_sh = jnp.where(row < d, 1.0, pltpu.roll(a, d, 0))
        b_sh = jnp.where(row < d, 0.0, pltpu.roll(b, d, 0))
        b = a * b_sh + b
        a = a * a_sh
        d *= 2
    h = b + a * h_ref[...]
    h_ref[...] = h[ts - 1:ts, :]
    o_ref[0] = (h * gy_ref[0].astype(F32)).astype(BF16)


def _rglru(xr, gy, cw, cb, wax, bax, lam):
    B, S, W = xr.shape
    ts = TS_LRU
    tok = pl.BlockSpec((1, ts, W), lambda b, s: (b, s, 0))
    return pl.pallas_call(
        _rglru_kernel,
        grid=(B, S // ts),
        in_specs=[tok, tok, _const_spec(cw.shape), _const_spec(cb.shape), _const_spec(wax.shape),
                  _const_spec(bax.shape), _const_spec(lam.shape)],
        out_specs=tok,
        out_shape=jax.ShapeDtypeStruct((B, S, W), BF16),
        scratch_shapes=[pltpu.VMEM((SUBLANES, W), F32), pltpu.VMEM((1, W), F32)],
        compiler_params=pltpu.CompilerParams(
            dimension_semantics=("parallel", "arbitrary"), vmem_limit_bytes=VMEM_LIMIT_BYTES),
        name="rglru",
    )(xr, gy, cw, cb, wax, bax, lam)


_NT = (((1,), (1,)), ((), ()))


def _nsa_kernel(q_ref, kc_ref, vo_ref, ks_ref, vst_ref, kw_ref, vwt_ref, g_ref, o_ref, bias_ref):
    i = pl.program_id(2)
    nq = HPG * Q_BLOCK
    q = q_ref[0, 0, 0]
    lane = lax.broadcasted_iota(jnp.int32, (1, nq), 1)
    t = i * Q_BLOCK + (lane & (Q_BLOCK - 1))

    n_cmp_pad = kc_ref.shape[2]
    sc = lax.dot_general(kc_ref[0, 0], q, _NT, preferred_element_type=F32)
    cmp_end = lax.broadcasted_iota(jnp.int32, (n_cmp_pad, 1), 0) * CMP_STRIDE + (CMP_BLOCK - 1)
    sc = jnp.where(cmp_end <= t, sc, NEG)
    e = jnp.exp(sc - jnp.max(sc, axis=0, keepdims=True))
    p = e / jnp.sum(e, axis=0, keepdims=True)
    p = jnp.where(t >= CMP_BLOCK - 1, p, 0.0)
    r = jnp.dot(vo_ref[0, 0], p.astype(BF16), preferred_element_type=F32)
    o_cmp = r[:HEAD_DIM]
    imp_h = r[HEAD_DIM:]
    n_slc = imp_h.shape[0]
    imp = imp_h[:, 0:Q_BLOCK]
    for h in range(1, HPG):
        imp = imp + imp_h[:, h * Q_BLOCK:(h + 1) * Q_BLOCK]

    tq = i * Q_BLOCK + lax.broadcasted_iota(jnp.int32, (1, Q_BLOCK), 1)
    cur = tq // SLC_BLOCK
    blk = lax.broadcasted_iota(jnp.int32, (n_slc, 1), 0)
    causal = blk <= cur
    forced = (blk == 0) | (blk == cur) | (blk == cur - 1)
    score = jnp.where(causal, jnp.where(forced, FORCE_SCORE, imp), NEG)
    rank = jnp.zeros((n_slc, Q_BLOCK), F32)
    for k in range(n_slc):
        sk = score[k:k + 1, :]
        ge = jnp.where(sk >= score, 1.0, 0.0)
        gt = jnp.where(sk > score, 1.0, 0.0)
        rank = rank + jnp.where(blk > k, ge, gt)
    sel = causal & (rank < float(min(N_SLC, n_slc)))
    bias = jnp.where(sel, 0.0, NEG)
    bias_ref[...] = jnp.concatenate([bias] * HPG, axis=1)

    init = (jnp.full((1, nq), NEG, F32), jnp.zeros((1, nq), F32), jnp.zeros((HEAD_DIM, nq), F32))

    def online_softmax_step(s, vt, carry):
        m, l, acc = carry
        m_new = jnp.maximum(m, jnp.max(s, axis=0, keepdims=True))
        alpha = jnp.exp(m - m_new)
        pj = jnp.exp(s - m_new)
        l = alpha * l + jnp.sum(pj, axis=0, keepdims=True)
        acc = alpha * acc + jnp.dot(vt, pj.astype(BF16), preferred_element_type=F32)
        return m_new, l, acc

    blocks_per_tile = TK_SLC // SLC_BLOCK
    row_s = lax.broadcasted_iota(jnp.int32, (TK_SLC, 1), 0)

    def slc_step(j, carry):
        off = pl.multiple_of(j * TK_SLC, TK_SLC)
        s = lax.dot_general(ks_ref[0, 0, pl.ds(off, TK_SLC), :], q, _NT, preferred_element_type=F32)
        b = bias_ref[pl.ds(pl.multiple_of(j * blocks_per_tile, blocks_per_tile), blocks_per_tile), :]
        s = (s.reshape(blocks_per_tile, SLC_BLOCK, nq) + b[:, None, :]).reshape(TK_SLC, nq)
        s = jnp.where(off + row_s <= t, s, NEG)
        return online_softmax_step(s, vst_ref[0, 0, :, pl.ds(off, TK_SLC)], carry)

    n_slc_tiles = (i * Q_BLOCK + Q_BLOCK - 1) // TK_SLC + 1
    _, l_s, acc_s = lax.fori_loop(0, n_slc_tiles, slc_step, init)

    row_w = lax.broadcasted_iota(jnp.int32, (TK_WIN, 1), 0)

    def win_step(j, carry):
        off = pl.multiple_of(j * TK_WIN, TK_WIN)
        s = lax.dot_general(kw_ref[0, 0, pl.ds(off, TK_WIN), :], q, _NT, preferred_element_type=F32)
        kp = off + row_w
        s = jnp.where(kp <= t, jnp.where(kp > t - WINDOW, s, NEG), NEG)
        return online_softmax_step(s, vwt_ref[0, 0, :, pl.ds(off, TK_WIN)], carry)

    first_w = jnp.maximum(i * Q_BLOCK - WINDOW, 0) // TK_WIN
    last_w = (i * Q_BLOCK + Q_BLOCK - 1) // TK_WIN
    _, l_w, acc_w = lax.fori_loop(first_w, last_w + 1, win_step, init)

    g = g_ref[0, 0, 0]
    o = g[0:1] * o_cmp + g[1:2] * (acc_s / l_s) + g[2:3] * (acc_w / l_w)
    o_ref[0, 0, 0] = o.astype(BF16)


def _nsa(q, kc, vo, ks, vst, kw, vwt, g):
    B, G, nqb, nq, dk = q.shape
    S = ks.shape[2]
    n_slc = S // SLC_BLOCK
    per_q = lambda shape: pl.BlockSpec((1, 1, 1) + shape, lambda b, gg, i: (b, gg, i, 0, 0))
    per_g = lambda shape: pl.BlockSpec((1, 1) + shape, lambda b, gg, i: (b, gg, 0, 0))
    return pl.pallas_call(
        _nsa_kernel,
        grid=(B, G, nqb),
        in_specs=[per_q((nq, dk)), per_g(kc.shape[2:]), per_g(vo.shape[2:]),
                  per_g((S, dk)), per_g((dk, S)), per_g((S, dk)), per_g((dk, S)),
                  per_q((N_BRANCH, nq))],
        out_specs=per_q((dk, nq)),
        out_shape=jax.ShapeDtypeStruct((B, G, nqb, dk, nq), BF16),
        scratch_shapes=[pltpu.VMEM((n_slc, nq), F32)],
        compiler_params=pltpu.CompilerParams(
            dimension_semantics=("parallel", "parallel", "arbitrary"), vmem_limit_bytes=VMEM_LIMIT_BYTES),
        name="nsa",
    )(q, kc, vo, ks, vst, kw, vwt, g)


def _out_kernel(x_ref, on_ref, ol_ref, gm_ref, wa_ref, wb_ref, wo_ref, n2_ref, w1_ref, w2_ref, o_ref):
    ua = jnp.dot(on_ref[...], wa_ref[...], preferred_element_type=F32)
    ub = jnp.dot(ol_ref[...], wb_ref[...], preferred_element_type=F32)
    gm = gm_ref[...]
    merged = gm[:, :D_MODEL].astype(F32) * ua + gm[:, D_MODEL:].astype(F32) * ub
    h = x_ref[...] + jnp.dot(merged.astype(BF16), wo_ref[...], preferred_element_type=F32)
    ms = jnp.mean(h * h, axis=-1, keepdims=True)
    hn = (h * lax.rsqrt(ms + EPS) * n2_ref[...]).astype(BF16)
    acc = h
    for c in range(D_FF // FF_CHUNK):
        u = jnp.dot(hn, w1_ref[:, c * FF_CHUNK:(c + 1) * FF_CHUNK], preferred_element_type=F32)
        u = jnp.square(jnp.maximum(u, 0.0)).astype(BF16)
        acc = acc + jnp.dot(u, w2_ref[c * FF_CHUNK:(c + 1) * FF_CHUNK, :], preferred_element_type=F32)
    o_ref[...] = acc


def _out(x2, on, ol, gm, wa, wb, wo, n2, w1, w2):
    T, D = x2.shape
    tm = TM_OUT
    tok = lambda width: pl.BlockSpec((tm, width), lambda i: (i, 0))
    return pl.pallas_call(
        _out_kernel,
        grid=(T // tm,),
        in_specs=[tok(D), tok(NSA_WIDTH), tok(LRU_WIDTH), tok(2 * D_MODEL),
                  _const_spec(wa.shape), _const_spec(wb.shape), _const_spec(wo.shape), _const_spec(n2.shape),
                  _const_spec(w1.shape), _const_spec(w2.shape)],
        out_specs=tok(D),
        out_shape=jax.ShapeDtypeStruct((T, D), F32),
        compiler_params=pltpu.CompilerParams(
            dimension_semantics=("parallel",), vmem_limit_bytes=VMEM_LIMIT_BYTES),
        name="out",
    )(x2, on, ol, gm, wa, wb, wo, n2, w1, w2)


def _rope_tables(pos):
    inv = ROPE_THETA ** (-jnp.arange(ROPE_HALF, dtype=F32) / ROPE_HALF)
    ang = pos.astype(F32)[:, None] * inv[None, :]
    cos, sin = jnp.cos(ang), jnp.sin(ang)
    n = pos.shape[0]
    rest = HEAD_DIM - ROPE_DIM
    c = jnp.concatenate([cos, cos, jnp.ones((n, rest), F32)], axis=1)
    sa = jnp.concatenate([-sin, jnp.zeros((n, HEAD_DIM - ROPE_HALF), F32)], axis=1)
    sb = jnp.concatenate([jnp.zeros((n, ROPE_HALF), F32), sin, jnp.zeros((n, rest), F32)], axis=1)
    two = lambda a: jnp.concatenate([a] * (LANES // HEAD_DIM), axis=1)
    return two(c), two(sa), two(sb)


def _layer(h, norm1_w, w_in, q_norm_w, k_norm_w, phi_k_pos, phi_k_w1, phi_k_w2, phi_v_pos, phi_v_w1, phi_v_w2,
           conv_w, conv_b, lru_wa, lru_ba, lru_wx, lru_bx, lru_lambda, w_nsa_up, w_lru_up, w_o, norm2_w,
           w_ff1, w_ff2):
    B, S, D = h.shape
    assert D == D_MODEL and S % TK_SLC == 0 and S % TM_IN == 0 and (B * S) % TM_OUT == 0
    nqb = S // Q_BLOCK
    n_chunks = S // CMP_STRIDE
    n_cmp = (S - CMP_BLOCK) // CMP_STRIDE + 1
    n_slc = S // SLC_BLOCK

    w_pad = jnp.concatenate(
        [w_in[:, :G_REAL_END], jnp.zeros((D, G_PAD - N_BRANCH * N_HEADS), w_in.dtype), w_in[:, G_REAL_END:]],
        axis=1).astype(BF16)
    qnw = jnp.tile(q_norm_w, N_HEADS)[None, :]
    ksw = jnp.tile(k_norm_w[1], N_KV)[None, :]
    kww = jnp.tile(k_norm_w[2], N_KV)[None, :]
    rc, ra, rb = _rope_tables(jnp.arange(S))
    seg = jnp.arange(NSA_WIDTH) // HEAD_DIM
    bd = (seg[:, None] == seg[None, :]).astype(BF16)

    q, kcv, ks, vs, kw, vw, g, xr, gy, gm = _in_proj(
        h, norm1_w[None, :], w_pad, qnw, ksw, kww, rc, ra, rb, bd)

    def chunks(raw):
        return raw.reshape(B, S, N_KV, HEAD_DIM).transpose(0, 2, 1, 3).reshape(
            B, N_KV, n_chunks, CMP_STRIDE * HEAD_DIM)

    pos2 = lambda p: p.reshape(2, CMP_STRIDE * HEAD_DIM)
    pad_w2 = lambda w: jnp.concatenate([w, jnp.zeros((PHI_HIDDEN, LANES - HEAD_DIM), w.dtype)], axis=1).astype(BF16)
    knw0 = jnp.concatenate([k_norm_w[0], jnp.zeros((LANES - HEAD_DIM,), F32)])[None, :]
    cmp_end = jnp.arange(n_chunks) * CMP_STRIDE + (CMP_BLOCK - 1)
    cc, ca, cb_ = _rope_tables(cmp_end)
    kc, vc = _compress(chunks(kcv[..., :KV_WIDTH]), chunks(kcv[..., KV_WIDTH:]),
                       pos2(phi_k_pos), pos2(phi_v_pos), phi_k_w1.astype(BF16), phi_v_w1.astype(BF16),
                       pad_w2(phi_k_w2), pad_w2(phi_v_w2), knw0, cc, ca, cb_)
    kc = kc[..., :HEAD_DIM]
    cmp_lo = jnp.arange(n_chunks) * CMP_STRIDE
    slc_lo = jnp.arange(n_slc) * SLC_BLOCK
    overlap_t = ((cmp_lo[None, :] <= slc_lo[:, None] + SLC_BLOCK - 1)
                 & (cmp_lo[None, :] + CMP_BLOCK - 1 >= slc_lo[:, None])
                 & (jnp.arange(n_chunks)[None, :] < n_cmp)).astype(BF16)
    vo = jnp.concatenate(
        [vc[..., :HEAD_DIM].transpose(0, 1, 3, 2), jnp.broadcast_to(overlap_t, (B, N_KV) + overlap_t.shape)],
        axis=2)

    wax = jnp.concatenate([lru_wa, lru_wx], axis=2).astype(BF16)
    bax = jnp.stack([lru_ba.reshape(LRU_WIDTH), lru_bx.reshape(LRU_WIDTH)])
    o_lru = _rglru(xr, gy, conv_w, conv_b[None, :], wax, bax, lru_lambda[None, :])

    heads = lambda a: a.reshape(B, S, N_KV, HEAD_DIM).transpose(0, 2, 1, 3)
    heads_t = lambda a: a.reshape(B, S, N_KV, HEAD_DIM).transpose(0, 2, 3, 1)
    q5 = q.reshape(B, nqb, Q_BLOCK, N_KV, HPG, HEAD_DIM).transpose(0, 3, 1, 4, 2, 5).reshape(
        B, N_KV, nqb, HPG * Q_BLOCK, HEAD_DIM)
    g5 = g[..., :N_BRANCH * N_HEADS].reshape(B, nqb, Q_BLOCK, N_BRANCH, N_KV, HPG).transpose(
        0, 4, 1, 3, 5, 2).reshape(B, N_KV, nqb, N_BRANCH, HPG * Q_BLOCK)
    o_t = _nsa(q5, kc, vo, heads(ks), heads_t(vs), heads(kw), heads_t(vw), g5)
    o_nsa = o_t.reshape(B, N_KV, nqb, HEAD_DIM, HPG, Q_BLOCK).transpose(0, 2, 5, 1, 4, 3).reshape(
        B * S, NSA_WIDTH)

    out = _out(h.reshape(B * S, D), o_nsa, o_lru.reshape(B * S, LRU_WIDTH), gm.reshape(B * S, 2 * D_MODEL),
               w_nsa_up.astype(BF16), w_lru_up.astype(BF16), w_o.astype(BF16), norm2_w[None, :],
               w_ff1.astype(BF16), w_ff2.astype(BF16))
    return out.reshape(B, S, D)


def kernel(x, norm1_w, w_in, q_norm_w, k_norm_w, phi_k_pos, phi_k_w1, phi_k_w2, phi_v_pos, phi_v_w1, phi_v_w2, conv_w, conv_b, lru_wa, lru_ba, lru_wx, lru_bx, lru_lambda, w_nsa_up, w_lru_up, w_o, norm2_w, w_ff1, w_ff2):
    params = (norm1_w, w_in, q_norm_w, k_norm_w, phi_k_pos, phi_k_w1, phi_k_w2, phi_v_pos, phi_v_w1, phi_v_w2,
              conv_w, conv_b, lru_wa, lru_ba, lru_wx, lru_bx, lru_lambda, w_nsa_up, w_lru_up, w_o, norm2_w,
              w_ff1, w_ff2)
    h = x
    for layer in range(w_in.shape[0]):
        h = _layer(h, *(p[layer] for p in params))
    return h
```

```python
import functools

import jax
import jax.numpy as jnp
from jax import lax
from jax.experimental import pallas as pl
from jax.experimental.pallas import tpu as pltpu

F32 = jnp.float32
BF16 = jnp.bfloat16

D_MODEL = 1024
N_HEADS = 8
N_KV = 2
HEAD_DIM = 64
HPG = N_HEADS // N_KV
NSA_WIDTH = N_HEADS * HEAD_DIM
KV_WIDTH = N_KV * HEAD_DIM
ROPE_DIM = HEAD_DIM // 4
ROPE_HALF = ROPE_DIM // 2
ROPE_THETA = 500000.0
CMP_BLOCK = 32
CMP_STRIDE = 16
SLC_BLOCK = 64
N_SLC = 16
WINDOW = 512
Q_BLOCK = 128
PHI_HIDDEN = 256
N_BRANCH = 3
LRU_WIDTH = D_MODEL
LRU_BLOCKS = 8
LRU_BD = LRU_WIDTH // LRU_BLOCKS
LRU_C = 8.0
CONV_W = 4
D_FF = 4 * D_MODEL
EPS = 1e-6
NEG = -1e30
FORCE_SCORE = 1e4
LOG2E = 1.4426950408889634

LANES = 128
SUBLANES = 8
VMEM_LIMIT_BYTES = 56 * 1024 * 1024

ONES_ROWS = 16
G_PAD = LANES
OFF_Q = 0
OFF_KC = OFF_Q + NSA_WIDTH
OFF_VC = OFF_KC + KV_WIDTH
OFF_KS = OFF_VC + KV_WIDTH
OFF_VS = OFF_KS + KV_WIDTH
OFF_KW = OFF_VS + KV_WIDTH
OFF_VW = OFF_KW + KV_WIDTH
OFF_G = OFF_VW + KV_WIDTH
OFF_XR = OFF_G + G_PAD
OFF_YR = OFF_XR + LRU_WIDTH
OFF_GM = OFF_YR + LRU_WIDTH
IN_COLS_PAD = OFF_GM + 2 * D_MODEL
G_REAL_END = NSA_WIDTH + 6 * KV_WIDTH + N_BRANCH * N_HEADS

TM_IN = 256
TS_LRU = 256
TM_OUT = 256
TK_SLC = 512
TK_WIN = 128
FF_CHUNK = 1024


def _const_spec(shape):
    nd = len(shape)
    return pl.BlockSpec(shape, lambda *_: (0,) * nd)


def _head_rmsnorm(v, w_tiled, bd):
    sq = v * v
    hi = sq.astype(BF16)
    lo = (sq - hi.astype(F32)).astype(BF16)
    ss = jnp.dot(hi, bd, preferred_element_type=F32) + jnp.dot(lo, bd, preferred_element_type=F32)
    return v * lax.rsqrt(ss * (1.0 / HEAD_DIM) + EPS) * w_tiled


def _rope(v, c, sa, sb):
    w = v.shape[-1]
    reps = w // LANES
    if reps > 1:
        c = jnp.concatenate([c] * reps, axis=1)
        sa = jnp.concatenate([sa] * reps, axis=1)
        sb = jnp.concatenate([sb] * reps, axis=1)
    up = pltpu.roll(v, w - ROPE_HALF, 1)
    dn = pltpu.roll(v, ROPE_HALF, 1)
    return v * c + up * sa + dn * sb


def _inproj_kernel(x_ref, n1_ref, w_ref, qnw_ref, ksw_ref, kww_ref, rc_ref, ra_ref, rb_ref, bd_ref,
                   q_ref, kcv_ref, ks_ref, vs_ref, kw_ref, vw_ref, g_ref, xr_ref, gy_ref, gm_ref):
    x = x_ref[0]
    ms = jnp.mean(x * x, axis=-1, keepdims=True)
    xn = (x * lax.rsqrt(ms + EPS) * n1_ref[...]).astype(BF16)

    def proj(off, width):
        return jnp.dot(xn, w_ref[:, off:off + width], preferred_element_type=F32)

    rc, ra, rb = rc_ref[...], ra_ref[...], rb_ref[...]
    bd = bd_ref[...]
    bd_kv = bd[:KV_WIDTH, :KV_WIDTH]

    q = _rope(_head_rmsnorm(proj(OFF_Q, NSA_WIDTH), qnw_ref[...], bd), rc, ra, rb)
    q_ref[0] = (q * (HEAD_DIM ** -0.5 * LOG2E)).astype(BF16)
    kcv_ref[0] = proj(OFF_KC, 2 * KV_WIDTH).astype(BF16)
    ks = _rope(_head_rmsnorm(proj(OFF_KS, KV_WIDTH), ksw_ref[...], bd_kv), rc, ra, rb)
    ks_ref[0] = ks.astype(BF16)
    vs_ref[0] = proj(OFF_VS, KV_WIDTH).astype(BF16)
    kw = _rope(_head_rmsnorm(proj(OFF_KW, KV_WIDTH), kww_ref[...], bd_kv), rc, ra, rb)
    kw_ref[0] = kw.astype(BF16)
    vw_ref[0] = proj(OFF_VW, KV_WIDTH).astype(BF16)
    g_ref[0] = jax.nn.sigmoid(proj(OFF_G, G_PAD))
    xr_ref[0] = proj(OFF_XR, LRU_WIDTH).astype(BF16)
    gy_ref[0] = jax.nn.gelu(proj(OFF_YR, LRU_WIDTH)).astype(BF16)
    gm_ref[0] = jax.nn.sigmoid(proj(OFF_GM, 2 * D_MODEL)).astype(BF16)


def _in_proj(x, n1, w_pad, qnw, ksw, kww, rc, ra, rb, bd):
    B, S, D = x.shape
    tm = TM_IN
    tok = lambda width: pl.BlockSpec((1, tm, width), lambda b, s: (b, s, 0))
    tab = pl.BlockSpec((tm, LANES), lambda b, s: (s, 0))
    out_widths = [NSA_WIDTH, 2 * KV_WIDTH, KV_WIDTH, KV_WIDTH, KV_WIDTH, KV_WIDTH, G_PAD,
                  LRU_WIDTH, LRU_WIDTH, 2 * D_MODEL]
    out_dtypes = [BF16, BF16, BF16, BF16, BF16, BF16, F32, BF16, BF16, BF16]
    return pl.pallas_call(
        _inproj_kernel,
        grid=(B, S // tm),
        in_specs=[tok(D), _const_spec((1, D)), _const_spec((D, IN_COLS_PAD)),
                  _const_spec((1, NSA_WIDTH)), _const_spec((1, KV_WIDTH)), _const_spec((1, KV_WIDTH)),
                  tab, tab, tab, _const_spec((NSA_WIDTH, NSA_WIDTH))],
        out_specs=[tok(w) for w in out_widths],
        out_shape=[jax.ShapeDtypeStruct((B, S, w), dt) for w, dt in zip(out_widths, out_dtypes)],
        compiler_params=pltpu.CompilerParams(
            dimension_semantics=("parallel", "parallel"), vmem_limit_bytes=VMEM_LIMIT_BYTES),
        name="in_proj",
    )(x, n1, w_pad, qnw, ksw, kww, rc, ra, rb, bd)


def _compress_kernel(xk_ref, xv_ref, pk_ref, pv_ref, w1k_ref, w1v_ref, w2k_ref, w2v_ref,
                     knw_ref, rc_ref, ra_ref, rb_ref, kc_ref, vc_ref):
    half = CMP_STRIDE * HEAD_DIM

    def mlp(x_ref, p_ref, w1_ref, w2_ref):
        x = x_ref[0, 0].astype(F32)
        lo = (x + p_ref[0:1, :]).astype(BF16)
        hi = (x + p_ref[1:2, :]).astype(BF16)
        a = jnp.dot(lo, w1_ref[:half, :], preferred_element_type=F32)
        b = jnp.dot(hi, w1_ref[half:, :], preferred_element_type=F32)
        hid = a + pltpu.roll(b, b.shape[0] - 1, 0)
        return jnp.dot(jax.nn.gelu(hid).astype(BF16), w2_ref[...], preferred_element_type=F32)

    kc = mlp(xk_ref, pk_ref, w1k_ref, w2k_ref)
    ms = jnp.sum(kc * kc, axis=-1, keepdims=True) * (1.0 / HEAD_DIM)
    kc = kc * lax.rsqrt(ms + EPS) * knw_ref[...]
    kc_ref[0, 0] = _rope(kc, rc_ref[...], ra_ref[...], rb_ref[...]).astype(BF16)
    vc_ref[0, 0] = mlp(xv_ref, pv_ref, w1v_ref, w2v_ref).astype(BF16)


def _compress(xk, xv, pk, pv, w1k, w1v, w2k, w2v, knw, rc, ra, rb):
    B, G, C, W = xk.shape
    blk = pl.BlockSpec((1, 1, C, W), lambda b, g: (b, g, 0, 0))
    oblk = pl.BlockSpec((1, 1, C, LANES), lambda b, g: (b, g, 0, 0))
    return pl.pallas_call(
        _compress_kernel,
        grid=(B, G),
        in_specs=[blk, blk, _const_spec(pk.shape), _const_spec(pv.shape),
                  _const_spec(w1k.shape), _const_spec(w1v.shape), _const_spec(w2k.shape), _const_spec(w2v.shape),
                  _const_spec(knw.shape), _const_spec(rc.shape), _const_spec(ra.shape), _const_spec(rb.shape)],
        out_specs=[oblk, oblk],
        out_shape=[jax.ShapeDtypeStruct((B, G, C, LANES), BF16)] * 2,
        compiler_params=pltpu.CompilerParams(
            dimension_semantics=("parallel", "parallel"), vmem_limit_bytes=VMEM_LIMIT_BYTES),
        name="compress",
    )(xk, xv, pk, pv, w1k, w1v, w2k, w2v, knw, rc, ra, rb)


def _rglru_kernel(xr_ref, gy_ref, cw_ref, cb_ref, wax_ref, bax_ref, lam_ref, o_ref, tail_ref, h_ref):
    ts = xr_ref.shape[1]

    @pl.when(pl.program_id(1) == 0)
    def _():
        tail_ref[...] = jnp.zeros_like(tail_ref)
        h_ref[...] = jnp.zeros_like(h_ref)

    x = xr_ref[0].astype(F32)
    xe = jnp.concatenate([tail_ref[...], x], axis=0)
    tail_ref[...] = x[ts - SUBLANES:, :]
    xc = cb_ref[...] + x * cw_ref[CONV_W - 1:CONV_W, :]
    for j in range(CONV_W - 1):
        back = CONV_W - 1 - j
        xc = xc + xe[SUBLANES - back:SUBLANES - back + ts, :] * cw_ref[j:j + 1, :]

    xcb = xc.astype(BF16)
    pre = [jnp.dot(xcb[:, n * LRU_BD:(n + 1) * LRU_BD], wax_ref[n], preferred_element_type=F32)
           for n in range(LRU_BLOCKS)]
    r = jax.nn.sigmoid(jnp.concatenate([p[:, :LRU_BD] for p in pre], axis=1) + bax_ref[0:1, :])
    ig = jax.nn.sigmoid(jnp.concatenate([p[:, LRU_BD:] for p in pre], axis=1) + bax_ref[1:2, :])

    lam = lam_ref[...]
    log_a = (-LRU_C * jax.nn.softplus(-lam)) * r
    a = jnp.exp(log_a)
    b = jnp.sqrt(-jnp.tanh(log_a) * (1.0 + a * a)) * (ig * xc)

    row = lax.broadcasted_iota(jnp.int32, (ts, 1), 0)
    d = 1
    while d < ts:
        a_sh = jnp.where(row < d, 1.0, pltpu.roll(a, d, 0))
        b_sh = jnp.where(row < d, 0.0, pltpu.roll(b, d, 0))
        b = a * b_sh + b
        a = a * a_sh
        d *= 2
    h = b + a * h_ref[...]
    h_ref[...] = h[ts - 1:ts, :]
    o_ref[0] = (h * gy_ref[0].astype(F32)).astype(BF16)


def _rglru(xr, gy, cw, cb, wax, bax, lam):
    B, S, W = xr.shape
    ts = TS_LRU
    tok = pl.BlockSpec((1, ts, W), lambda b, s: (b, s, 0))
    return pl.pallas_call(
        _rglru_kernel,
        grid=(B, S // ts),
        in_specs=[tok, tok, _const_spec(cw.shape), _const_spec(cb.shape), _const_spec(wax.shape),
                  _const_spec(bax.shape), _const_spec(lam.shape)],
        out_specs=tok,
        out_shape=jax.ShapeDtypeStruct((B, S, W), BF16),
        scratch_shapes=[pltpu.VMEM((SUBLANES, W), F32), pltpu.VMEM((1, W), F32)],
        compiler_params=pltpu.CompilerParams(
            dimension_semantics=("parallel", "arbitrary"), vmem_limit_bytes=VMEM_LIMIT_BYTES),
        name="rglru",
    )(xr, gy, cw, cb, wax, bax, lam)


_NT = (((1,), (1,)), ((), ()))


def _select_bias(imp, cur, n_sel):
    n_slc, nqb = imp.shape
    blk = lax.broadcasted_iota(jnp.int32, (n_slc, 1), 0)
    causal = blk <= cur
    forced = (blk == 0) | (blk == cur) | (blk == cur - 1)
    score = jnp.where(causal, jnp.where(forced, FORCE_SCORE, imp), NEG)
    n_ch = n_slc // SUBLANES
    chunks = [score[c * SUBLANES:(c + 1) * SUBLANES] for c in range(n_ch)]
    ranks = [jnp.zeros((SUBLANES, nqb), F32) for _ in range(n_ch)]
    sub = lax.broadcasted_iota(jnp.int32, (SUBLANES, 1), 0)
    for k in range(n_slc):
        sk = jnp.broadcast_to(score[k:k + 1, :], (SUBLANES, nqb))
        k_chunk, k_row = divmod(k, SUBLANES)
        for c in range(n_ch):
            if c > k_chunk:
                one = jnp.where(sk >= chunks[c], 1.0, 0.0)
            elif c < k_chunk:
                one = jnp.where(sk > chunks[c], 1.0, 0.0)
            else:
                one = jnp.where(sub > k_row, jnp.where(sk >= chunks[c], 1.0, 0.0),
                                jnp.where(sk > chunks[c], 1.0, 0.0))
            ranks[c] = ranks[c] + one
    rank = jnp.concatenate(ranks, axis=0)
    return jnp.where(causal & (rank < float(n_sel)), 0.0, NEG)


def _nsa_kernel(q_ref, kc_ref, vo_ref, ks_ref, vst_ref, kw_ref, vwt_ref, g_ref, o_ref, bp_ref, qa_ref, s_ref):
    i = pl.program_id(1)
    n_grp = q_ref.shape[1]
    nq = HPG * Q_BLOCK
    lane = lax.broadcasted_iota(jnp.int32, (1, nq), 1)
    tl = lane & (Q_BLOCK - 1)
    t = i * Q_BLOCK + tl
    tq = i * Q_BLOCK + lax.broadcasted_iota(jnp.int32, (1, Q_BLOCK), 1)
    cur = lax.shift_right_logical(tq, SLC_BLOCK.bit_length() - 1)
    n_cmp_pad = kc_ref.shape[2]
    n_slc = vo_ref.shape[2] - HEAD_DIM - ONES_ROWS
    cmp_end = lax.broadcasted_iota(jnp.int32, (n_cmp_pad, 1), 0) * CMP_STRIDE + (CMP_BLOCK - 1)
    lane_q = lax.broadcasted_iota(jnp.int32, (nq, LANES), 1)
    row_q = lax.broadcasted_iota(jnp.int32, (Q_BLOCK, 1), 0)
    wk = WINDOW + Q_BLOCK
    w_off = pl.multiple_of(i * Q_BLOCK, Q_BLOCK)

    groups = range(n_grp)
    qs = [q_ref[0, g, 0] for g in groups]
    dot_nt = lambda a, b: lax.dot_general(a, b, _NT, preferred_element_type=F32)
    dot_nn = lambda a, b: jnp.dot(a, b, preferred_element_type=F32)


    sc = [dot_nt(kc_ref[0, g], qs[g]) for g in groups]
    sw = [dot_nt(kw_ref[0, g, pl.ds(w_off, wk), :], qs[g]) for g in groups]
    pc = []
    for g in groups:
        s = jnp.where(cmp_end <= t, sc[g], NEG)
        pc.append(jnp.exp2(s - jnp.max(s, axis=0, keepdims=True)).astype(BF16))
    n_val = HEAD_DIM + n_slc
    rc = []
    for g in groups:
        r = dot_nn(vo_ref[0, g], pc[g])
        r = r[:n_val] / r[n_val:n_val + 1]
        rc.append(jnp.where(t >= CMP_BLOCK - 1, r, 0.0))
    pw = []
    for g in groups:
        s = sw[g]
        s = jnp.concatenate([jnp.where(row_q > tl, s[:Q_BLOCK], NEG), s[Q_BLOCK:WINDOW],
                             jnp.where(row_q <= tl, s[WINDOW:], NEG)], axis=0)
        pw.append(jnp.exp2(s - jnp.max(s, axis=0, keepdims=True)).astype(BF16))
    o_win = []
    for g in groups:
        r = dot_nn(vwt_ref[0, g, :, pl.ds(w_off, wk)], pw[g])
        o_win.append(r[:HEAD_DIM] / r[HEAD_DIM:HEAD_DIM + 1])

    for g in groups:
        imp_h = rc[g][HEAD_DIM:]
        imp = imp_h[:, 0:Q_BLOCK]
        for h in range(1, HPG):
            imp = imp + imp_h[:, h * Q_BLOCK:(h + 1) * Q_BLOCK]
        bp_ref[g] = jnp.zeros(bp_ref.shape[1:], F32)
        bp_ref[g, HEAD_DIM:HEAD_DIM + n_slc, :] = _select_bias(imp, cur, min(N_SLC, n_slc))
        bt = bp_ref[g].T.astype(BF16)
        qa_ref[g] = jnp.where(lane_q < HEAD_DIM, qs[g], jnp.concatenate([bt] * HPG, axis=0))

    n_full = lax.shift_right_logical(i * Q_BLOCK, TK_SLC.bit_length() - 1)
    n_pairs = lax.shift_right_logical(n_full, 1) + 1

    def key_off(u):
        tile = jnp.where(u == 0, n_full, jnp.where(u <= n_full, u - 1, u))
        return pl.multiple_of(tile * TK_SLC, TK_SLC)

    def scores(u, slot, masked):
        off = key_off(u)
        for g in groups:
            s = dot_nt(ks_ref[0, g, pl.ds(off, TK_SLC), :], qa_ref[g])
            if masked:
                s = jnp.where(off + lax.broadcasted_iota(jnp.int32, (TK_SLC, 1), 0) <= t, s, NEG)
            s_ref[slot, g] = s

    def consume(u, slot, carry):
        off = key_off(u)
        ps, stats = [], []
        for g in groups:
            m, acc = carry[g]
            s = s_ref[slot, g]
            m_new = jnp.maximum(m, jnp.max(s, axis=0, keepdims=True))
            stats.append((m_new, jnp.exp2(m - m_new) * acc))
            ps.append(jnp.exp2(s - m_new).astype(BF16))
        pv = [dot_nn(vst_ref[0, g, :, pl.ds(off, TK_SLC)], ps[g]) for g in groups]
        return tuple((stats[g][0], stats[g][1] + pv[g]) for g in groups)

    def pair_step(j, carry):
        scores(2 * j + 1, 1, False)
        carry = consume(2 * j, 0, carry)
        scores(2 * j + 2, 0, False)
        return consume(2 * j + 1, 1, carry)

    init = (jnp.full((1, nq), NEG, F32), jnp.zeros((vst_ref.shape[2], nq), F32))
    scores(0, 0, True)
    carry = lax.fori_loop(0, n_pairs - 1, pair_step, (init,) * n_grp)
    last = 2 * n_pairs - 1
    scores(last, 1, False)
    carry = consume(last - 1, 0, carry)
    carry = consume(last, 1, carry)

    for g in groups:
        _, acc_s = carry[g]
        o_slc = acc_s[:HEAD_DIM] / acc_s[HEAD_DIM:HEAD_DIM + 1]
        gate = g_ref[0, g, 0]
        o = gate[0:1] * rc[g][:HEAD_DIM] + gate[1:2] * o_slc + gate[2:3] * o_win[g]
        o_ref[0, g, 0] = o.astype(BF16)


def _nsa(q, kc, vo, ks, vst, kw, vwt, g):
    B, G, nqb, nq, _ = q.shape
    S = ks.shape[2]
    dk = HEAD_DIM
    n_slc = S // SLC_BLOCK
    assert HEAD_DIM + n_slc <= LANES and (S // TK_SLC) % 2 == 0
    per_q =lambda shape: pl.BlockSpec((1, G, 1) + shape, lambda b, i: (b, 0, i, 0, 0))
    per_b = lambda a: pl.BlockSpec((1,) + a.shape[1:], lambda b, i: (b, 0, 0, 0))
    return pl.pallas_call(
        _nsa_kernel,
        grid=(B, nqb),
        in_specs=[per_q((nq, LANES)), per_b(kc), per_b(vo), per_b(ks), per_b(vst), per_b(kw), per_b(vwt),
                  per_q((N_BRANCH, nq))],
        out_specs=per_q((dk, nq)),
        out_shape=jax.ShapeDtypeStruct((B, G, nqb, dk, nq), BF16),
        scratch_shapes=[pltpu.VMEM((G, LANES, Q_BLOCK), F32), pltpu.VMEM((G, nq, LANES), BF16),
                        pltpu.VMEM((2, G, TK_SLC, nq), F32)],
        compiler_params=pltpu.CompilerParams(
            dimension_semantics=("parallel", "arbitrary"), vmem_limit_bytes=VMEM_LIMIT_BYTES),
        name="nsa",
    )(q, kc, vo, ks, vst, kw, vwt, g)


def _out_kernel(x_ref, on_ref, ol_ref, gm_ref, wa_ref, wb_ref, wo_ref, n2_ref, w1_ref, w2_ref, o_ref):
    ua = jnp.dot(on_ref[...], wa_ref[...], preferred_element_type=F32)
    ub = jnp.dot(ol_ref[...], wb_ref[...], preferred_element_type=F32)
    gm = gm_ref[...]
    merged = gm[:, :D_MODEL].astype(F32) * ua + gm[:, D_MODEL:].astype(F32) * ub
    h = x_ref[...] + jnp.dot(merged.astype(BF16), wo_ref[...], preferred_element_type=F32)
    ms = jnp.mean(h * h, axis=-1, keepdims=True)
    hn = (h * lax.rsqrt(ms + EPS) * n2_ref[...]).astype(BF16)
    acc = h
    for c in range(D_FF // FF_CHUNK):
        u = jnp.dot(hn, w1_ref[:, c * FF_CHUNK:(c + 1) * FF_CHUNK], preferred_element_type=F32)
        u = jnp.square(jnp.maximum(u, 0.0)).astype(BF16)
        acc = acc + jnp.dot(u, w2_ref[c * FF_CHUNK:(c + 1) * FF_CHUNK, :], preferred_element_type=F32)
    o_ref[...] = acc


def _out(x2, on, ol, gm, wa, wb, wo, n2, w1, w2):
    T, D = x2.shape
    tm = TM_OUT
    tok = lambda width: pl.BlockSpec((tm, width), lambda i: (i, 0))
    return pl.pallas_call(
        _out_kernel,
        grid=(T // tm,),
        in_specs=[tok(D), tok(NSA_WIDTH), tok(LRU_WIDTH), tok(2 * D_MODEL),
                  _const_spec(wa.shape), _const_spec(wb.shape), _const_spec(wo.shape), _const_spec(n2.shape),
                  _const_spec(w1.shape), _const_spec(w2.shape)],
        out_specs=tok(D),
        out_shape=jax.ShapeDtypeStruct((T, D), F32),
        compiler_params=pltpu.CompilerParams(
            dimension_semantics=("parallel",), vmem_limit_bytes=VMEM_LIMIT_BYTES),
        name="out",
    )(x2, on, ol, gm, wa, wb, wo, n2, w1, w2)


def _rope_tables(pos):
    inv = ROPE_THETA ** (-jnp.arange(ROPE_HALF, dtype=F32) / ROPE_HALF)
    ang = pos.astype(F32)[:, None] * inv[None, :]
    cos, sin = jnp.cos(ang), jnp.sin(ang)
    n = pos.shape[0]
    rest = HEAD_DIM - ROPE_DIM
    c = jnp.concatenate([cos, cos, jnp.ones((n, rest), F32)], axis=1)
    sa = jnp.concatenate([-sin, jnp.zeros((n, HEAD_DIM - ROPE_HALF), F32)], axis=1)
    sb = jnp.concatenate([jnp.zeros((n, ROPE_HALF), F32), sin, jnp.zeros((n, rest), F32)], axis=1)
    two = lambda a: jnp.concatenate([a] * (LANES // HEAD_DIM), axis=1)
    return two(c), two(sa), two(sb)


def _layer(h, norm1_w, w_in, q_norm_w, k_norm_w, phi_k_pos, phi_k_w1, phi_k_w2, phi_v_pos, phi_v_w1, phi_v_w2,
           conv_w, conv_b, lru_wa, lru_ba, lru_wx, lru_bx, lru_lambda, w_nsa_up, w_lru_up, w_o, norm2_w,
           w_ff1, w_ff2):
    B, S, D = h.shape
    assert D == D_MODEL and S % TK_SLC == 0 and S % TM_IN == 0 and (B * S) % TM_OUT == 0
    nqb = S // Q_BLOCK
    n_chunks = S // CMP_STRIDE
    n_cmp = (S - CMP_BLOCK) // CMP_STRIDE + 1
    n_slc = S // SLC_BLOCK

    w_pad = jnp.concatenate(
        [w_in[:, :G_REAL_END], jnp.zeros((D, G_PAD - N_BRANCH * N_HEADS), w_in.dtype), w_in[:, G_REAL_END:]],
        axis=1).astype(BF16)
    qnw = jnp.tile(q_norm_w, N_HEADS)[None, :]
    ksw = jnp.tile(k_norm_w[1], N_KV)[None, :]
    kww = jnp.tile(k_norm_w[2], N_KV)[None, :]
    rc, ra, rb = _rope_tables(jnp.arange(S))
    seg = jnp.arange(NSA_WIDTH) // HEAD_DIM
    bd = (seg[:, None] == seg[None, :]).astype(BF16)

    q, kcv, ks, vs, kw, vw, g, xr, gy, gm = _in_proj(
        h, norm1_w[None, :], w_pad, qnw, ksw, kww, rc, ra, rb, bd)

    def chunks(raw):
        return raw.reshape(B, S, N_KV, HEAD_DIM).transpose(0, 2, 1, 3).reshape(
            B, N_KV, n_chunks, CMP_STRIDE * HEAD_DIM)

    pos2 = lambda p: p.reshape(2, CMP_STRIDE * HEAD_DIM)
    pad_w2 = lambda w: jnp.concatenate([w, jnp.zeros((PHI_HIDDEN, LANES - HEAD_DIM), w.dtype)], axis=1).astype(BF16)
    knw0 = jnp.concatenate([k_norm_w[0], jnp.zeros((LANES - HEAD_DIM,), F32)])[None, :]
    cmp_end = jnp.arange(n_chunks) * CMP_STRIDE + (CMP_BLOCK - 1)
    cc, ca, cb_ = _rope_tables(cmp_end)
    kc, vc = _compress(chunks(kcv[..., :KV_WIDTH]), chunks(kcv[..., KV_WIDTH:]),
                       pos2(phi_k_pos), pos2(phi_v_pos), phi_k_w1.astype(BF16), phi_v_w1.astype(BF16),
                       pad_w2(phi_k_w2), pad_w2(phi_v_w2), knw0, cc, ca, cb_)
    cmp_lo = jnp.arange(n_chunks) * CMP_STRIDE
    slc_lo = jnp.arange(n_slc) * SLC_BLOCK
    overlap_t = ((cmp_lo[None, :] <= slc_lo[:, None] + SLC_BLOCK - 1)
                 & (cmp_lo[None, :] + CMP_BLOCK - 1 >= slc_lo[:, None])
                 & (jnp.arange(n_chunks)[None, :] < n_cmp)).astype(BF16)
    ones_rows = lambda n: jnp.ones((B, N_KV, ONES_ROWS, n), BF16)
    vo = jnp.concatenate(
        [vc[..., :HEAD_DIM].transpose(0, 1, 3, 2), jnp.broadcast_to(overlap_t, (B, N_KV) + overlap_t.shape),
         ones_rows(n_chunks)], axis=2)

    wax = jnp.concatenate([lru_wa, lru_wx], axis=2).astype(BF16)
    bax = jnp.stack([lru_ba.reshape(LRU_WIDTH), lru_bx.reshape(LRU_WIDTH)])
    o_lru = _rglru(xr, gy, conv_w, conv_b[None, :], wax, bax, lru_lambda[None, :])

    heads = lambda a: a.reshape(B, S, N_KV, HEAD_DIM).transpose(0, 2, 1, 3)
    heads_t = lambda a: a.reshape(B, S, N_KV, HEAD_DIM).transpose(0, 2, 3, 1)
    q5 = q.reshape(B, nqb, Q_BLOCK, N_KV, HPG, HEAD_DIM).transpose(0, 3, 1, 4, 2, 5).reshape(
        B, N_KV, nqb, HPG * Q_BLOCK, HEAD_DIM)
    pad_lanes = LANES - HEAD_DIM
    flag = (jnp.arange(pad_lanes) == 0).astype(F32)
    q5 = jnp.concatenate([q5, jnp.broadcast_to((NEG * flag).astype(BF16), q5.shape[:-1] + (pad_lanes,))], axis=-1)
    g5 = g[..., :N_BRANCH * N_HEADS].reshape(B, nqb, Q_BLOCK, N_BRANCH, N_KV, HPG).transpose(
        0, 4, 1, 3, 5, 2).reshape(B, N_KV, nqb, N_BRANCH, HPG * Q_BLOCK)
    onehot = (jnp.arange(S)[:, None] // SLC_BLOCK == jnp.arange(pad_lanes)[None, :]).astype(BF16)
    ks_aug = jnp.concatenate([heads(ks), jnp.broadcast_to(onehot, (B, N_KV) + onehot.shape)], axis=-1)
    kw_real = jnp.pad(heads(kw), ((0, 0), (0, 0), (0, 0), (0, pad_lanes)))
    kw_pad = jnp.broadcast_to(jnp.concatenate([jnp.zeros((HEAD_DIM,), F32), flag]).astype(BF16),
                              (B, N_KV, WINDOW, LANES))
    kw_aug = jnp.concatenate([kw_pad, kw_real], axis=2)
    vwt_aug = jnp.concatenate([jnp.pad(heads_t(vw), ((0, 0), (0, 0), (0, 0), (WINDOW, 0))),
                               ones_rows(WINDOW + S)], axis=2)
    vst_aug = jnp.concatenate([heads_t(vs), ones_rows(S)], axis=2)
    o_t = _nsa(q5, kc, vo, ks_aug, vst_aug, kw_aug, vwt_aug, g5)
    o_nsa = o_t.reshape(B, N_KV, nqb, HEAD_DIM, HPG, Q_BLOCK).transpose(0, 2, 5, 1, 4, 3).reshape(
        B * S, NSA_WIDTH)

    out = _out(h.reshape(B * S, D), o_nsa, o_lru.reshape(B * S, LRU_WIDTH), gm.reshape(B * S, 2 * D_MODEL),
               w_nsa_up.astype(BF16), w_lru_up.astype(BF16), w_o.astype(BF16), norm2_w[None, :],
               w_ff1.astype(BF16), w_ff2.astype(BF16))
    return out.reshape(B, S, D)


def kernel(x, norm1_w, w_in, q_norm_w, k_norm_w, phi_k_pos, phi_k_w1, phi_k_w2, phi_v_pos, phi_v_w1, phi_v_w2, conv_w, conv_b, lru_wa, lru_ba, lru_wx, lru_bx, lru_lambda, w_nsa_up, w_lru_up, w_o, norm2_w, w_ff1, w_ff2):
    params = (norm1_w, w_in, q_norm_w, k_norm_w, phi_k_pos, phi_k_w1, phi_k_w2, phi_v_pos, phi_v_w1, phi_v_w2,
              conv_w, conv_b, lru_wa, lru_ba, lru_wx, lru_bx, lru_lambda, w_nsa_up, w_lru_up, w_o, norm2_w,
              w_ff1, w_ff2)
    h = x
    for layer in range(w_in.shape[0]):
        h = _layer(h, *(p[layer] for p in params))
    return h
```

```python
import jax
import jax.numpy as jnp
from jax import lax
from jax.experimental import pallas as pl
from jax.experimental.pallas import tpu as pltpu

F32 = jnp.float32
BF16 = jnp.bfloat16

D_MODEL = 1024
N_HEADS = 8
N_KV = 2
HEAD_DIM = 64
HPG = N_HEADS // N_KV
NSA_WIDTH = N_HEADS * HEAD_DIM
KV_WIDTH = N_KV * HEAD_DIM
ROPE_DIM = HEAD_DIM // 4
ROPE_HALF = ROPE_DIM // 2
ROPE_THETA = 500000.0
CMP_BLOCK = 32
CMP_STRIDE = 16
SLC_BLOCK = 64
N_SLC = 16
WINDOW = 512
Q_BLOCK = 128
PHI_HIDDEN = 256
N_BRANCH = 3
LRU_WIDTH = D_MODEL
LRU_BLOCKS = 8
LRU_BD = LRU_WIDTH // LRU_BLOCKS
LRU_C = 8.0
CONV_W = 4
D_FF = 4 * D_MODEL
EPS = 1e-6
NEG = -1e30
FORCE_SCORE = 1e4
LOG2E = 1.4426950408889634

LANES = 128
SUBLANES = 8
VMEM_LIMIT_BYTES = 56 * 1024 * 1024

ONES_ROWS = 16
G_PAD = LANES
OFF_Q = 0
OFF_KC = OFF_Q + NSA_WIDTH
OFF_VC = OFF_KC + KV_WIDTH
OFF_KS = OFF_VC + KV_WIDTH
OFF_VS = OFF_KS + KV_WIDTH
OFF_KW = OFF_VS + KV_WIDTH
OFF_VW = OFF_KW + KV_WIDTH
OFF_G = OFF_VW + KV_WIDTH
OFF_XR = OFF_G + G_PAD
OFF_YR = OFF_XR + LRU_WIDTH
OFF_GM = OFF_YR + LRU_WIDTH
IN_COLS_PAD = OFF_GM + 2 * D_MODEL
G_REAL_END = NSA_WIDTH + 6 * KV_WIDTH + N_BRANCH * N_HEADS

TM_IN = 512
TS_LRU = 64
TM_OUT = 512
TK_SLC = 512
FF_CHUNK = 1024


def _const_spec(shape):
    nd = len(shape)
    return pl.BlockSpec(shape, lambda *_: (0,) * nd, pipeline_mode=pl.Buffered(1))


def _head_rmsnorm(v, w_tiled, bd):
    sq = v * v
    hi = sq.astype(BF16)
    lo = (sq - hi.astype(F32)).astype(BF16)
    ss = jnp.dot(hi, bd, preferred_element_type=F32) + jnp.dot(lo, bd, preferred_element_type=F32)
    return v * lax.rsqrt(ss * (1.0 / HEAD_DIM) + EPS) * w_tiled


def _rope(v, c, sa, sb):
    w = v.shape[-1]
    reps = w // LANES
    if reps > 1:
        c = jnp.concatenate([c] * reps, axis=1)
        sa = jnp.concatenate([sa] * reps, axis=1)
        sb = jnp.concatenate([sb] * reps, axis=1)
    up = pltpu.roll(v, w - ROPE_HALF, 1)
    dn = pltpu.roll(v, ROPE_HALF, 1)
    return v * c + up * sa + dn * sb


def _inproj_kernel(x_ref, n1_ref, w_ref, wt_ref, qnw_ref, ksw_ref, kww_ref, rc_ref, ra_ref, rb_ref, bd_ref,
                   kw_base_ref, vw_base_ref,
                   q_ref, kcv_ref, ks_ref, vs_ref, kw_ref, vw_ref, g_ref, xr_ref, gy_ref, gm_ref):
    del kw_base_ref, vw_base_ref
    tm = x_ref.shape[1]
    n_grp = ks_ref.shape[1]
    x = x_ref[0]
    ms = jnp.mean(x * x, axis=-1, keepdims=True)
    xn = (x * lax.rsqrt(ms + EPS) * n1_ref[...]).astype(BF16)

    def proj(off, width):
        return jnp.dot(xn, w_ref[:, off:off + width], preferred_element_type=F32)

    rc, ra, rb = rc_ref[...], ra_ref[...], rb_ref[...]
    bd = bd_ref[...]
    bd_kv = bd[:KV_WIDTH, :KV_WIDTH]
    lane = lax.broadcasted_iota(jnp.int32, (tm, LANES), 1)
    low = lane < HEAD_DIM

    def head_tile(v, head):
        tile = v[:, (head // 2) * LANES:(head // 2 + 1) * LANES]
        return pltpu.roll(tile, HEAD_DIM, 1) if head % 2 else tile

    q = _rope(_head_rmsnorm(proj(OFF_Q, NSA_WIDTH), qnw_ref[...], bd), rc, ra, rb)
    q = q * (HEAD_DIM ** -0.5 * LOG2E)
    q_pad = jnp.where(lane == HEAD_DIM, NEG, 0.0)
    for head in range(N_HEADS):
        g, h = divmod(head, HPG)
        v = jnp.where(low, head_tile(q, head), q_pad).astype(BF16)
        for qb in range(tm // Q_BLOCK):
            q_ref[0, g, qb, h * Q_BLOCK:(h + 1) * Q_BLOCK, :] = v[qb * Q_BLOCK:(qb + 1) * Q_BLOCK]

    kcv_ref[0] = proj(OFF_KC, 2 * KV_WIDTH).astype(BF16)

    ks = _rope(_head_rmsnorm(proj(OFF_KS, KV_WIDTH), ksw_ref[...], bd_kv), rc, ra, rb)
    pos = pl.program_id(1) * tm + lax.broadcasted_iota(jnp.int32, (tm, 1), 0)
    block = lax.shift_right_logical(pos, SLC_BLOCK.bit_length() - 1)
    onehot = jnp.where(lane - HEAD_DIM == block, 1.0, 0.0)
    kw = _rope(_head_rmsnorm(proj(OFF_KW, KV_WIDTH), kww_ref[...], bd_kv), rc, ra, rb)
    for g in range(n_grp):
        ks_ref[0, g] = jnp.where(low, head_tile(ks, g), onehot).astype(BF16)
        kw_ref[0, g] = jnp.where(low, head_tile(kw, g), 0.0).astype(BF16)

    t_out = lax.dot_general(wt_ref[...], xn, _NT, preferred_element_type=F32)
    ones = jnp.ones((ONES_ROWS, tm), BF16)
    for g in range(n_grp):
        rows = slice(g * HEAD_DIM, (g + 1) * HEAD_DIM)
        vs_ref[0, g] = jnp.concatenate([t_out[:KV_WIDTH][rows].astype(BF16), ones], axis=0)
        vw_ref[0, g] = jnp.concatenate([t_out[KV_WIDTH:2 * KV_WIDTH][rows].astype(BF16), ones], axis=0)
    g_ref[0] = 0.5 * jnp.tanh(t_out[2 * KV_WIDTH:]) + 0.5
    xr_ref[0] = proj(OFF_XR, LRU_WIDTH).astype(BF16)
    gy_ref[0] = jax.nn.gelu(proj(OFF_YR, LRU_WIDTH)).astype(BF16)
    gm_ref[0] = (0.5 * jnp.tanh(proj(OFF_GM, 2 * D_MODEL)) + 0.5).astype(BF16)


GATE_ROWS = 32


def _in_proj(x, n1, w_pad, w_t, qnw, ksw, kww, rc, ra, rb, bd):
    B, S, D = x.shape
    tm = TM_IN
    G = N_KV
    assert WINDOW % tm == 0 and tm % Q_BLOCK == 0
    pad_blocks = WINDOW // tm
    v_rows = HEAD_DIM + ONES_ROWS
    kw_base = jnp.broadcast_to((jnp.arange(LANES) == HEAD_DIM).astype(BF16), (B, G, WINDOW + S, LANES))
    vw_base = jnp.zeros((B, G, v_rows, WINDOW + S), BF16)

    tok = lambda width: pl.BlockSpec((1, tm, width), lambda b, s: (b, s, 0))
    tab = pl.BlockSpec((tm, LANES), lambda b, s: (s, 0))
    any_spec = pl.BlockSpec(memory_space=pl.ANY)
    out_specs = [
        pl.BlockSpec((1, G, tm // Q_BLOCK, HPG * Q_BLOCK, LANES), lambda b, s: (b, 0, s, 0, 0)),
        tok(2 * KV_WIDTH),
        pl.BlockSpec((1, G, tm, LANES), lambda b, s: (b, 0, s, 0)),
        pl.BlockSpec((1, G, v_rows, tm), lambda b, s: (b, 0, 0, s)),
        pl.BlockSpec((1, G, tm, LANES), lambda b, s: (b, 0, s + pad_blocks, 0)),
        pl.BlockSpec((1, G, v_rows, tm), lambda b, s: (b, 0, 0, s + pad_blocks)),
        pl.BlockSpec((1, GATE_ROWS, tm), lambda b, s: (b, 0, s)),
        tok(LRU_WIDTH), tok(LRU_WIDTH), tok(2 * D_MODEL)]
    out_shape = [
        jax.ShapeDtypeStruct((B, G, S // Q_BLOCK, HPG * Q_BLOCK, LANES), BF16),
        jax.ShapeDtypeStruct((B, S, 2 * KV_WIDTH), BF16),
        jax.ShapeDtypeStruct((B, G, S, LANES), BF16),
        jax.ShapeDtypeStruct((B, G, v_rows, S), BF16),
        jax.ShapeDtypeStruct(kw_base.shape, BF16),
        jax.ShapeDtypeStruct(vw_base.shape, BF16),
        jax.ShapeDtypeStruct((B, GATE_ROWS, S), F32),
        jax.ShapeDtypeStruct((B, S, LRU_WIDTH), BF16),
        jax.ShapeDtypeStruct((B, S, LRU_WIDTH), BF16),
        jax.ShapeDtypeStruct((B, S, 2 * D_MODEL), BF16)]
    return pl.pallas_call(
        _inproj_kernel,
        grid=(B, S // tm),
        in_specs=[tok(D), _const_spec((1, D)), _const_spec((D, IN_COLS_PAD)), _const_spec(w_t.shape),
                  _const_spec((1, NSA_WIDTH)), _const_spec((1, KV_WIDTH)), _const_spec((1, KV_WIDTH)),
                  tab, tab, tab, _const_spec((NSA_WIDTH, NSA_WIDTH)), any_spec, any_spec],
        out_specs=out_specs,
        out_shape=out_shape,
        input_output_aliases={11: 4, 12: 5},
        compiler_params=pltpu.CompilerParams(
            dimension_semantics=("parallel", "parallel"), vmem_limit_bytes=VMEM_LIMIT_BYTES),
        name="in_proj",
    )(x, n1, w_pad, w_t, qnw, ksw, kww, rc, ra, rb, bd, kw_base, vw_base)


def _compress_kernel(xk_ref, xv_ref, pk_ref, pv_ref, w1k_ref, w1v_ref, w2k_ref, w2v_ref,
                     knw_ref, rc_ref, ra_ref, rb_ref, kc_ref, vc_ref):
    half = CMP_STRIDE * HEAD_DIM

    def mlp(x_ref, p_ref, w1_ref, w2_ref):
        x = x_ref[0, 0].astype(F32)
        lo = (x + p_ref[0:1, :]).astype(BF16)
        hi = (x + p_ref[1:2, :]).astype(BF16)
        a = jnp.dot(lo, w1_ref[:half, :], preferred_element_type=F32)
        b = jnp.dot(hi, w1_ref[half:, :], preferred_element_type=F32)
        hid = a + pltpu.roll(b, b.shape[0] - 1, 0)
        return jnp.dot(jax.nn.gelu(hid).astype(BF16), w2_ref[...], preferred_element_type=F32)

    kc = mlp(xk_ref, pk_ref, w1k_ref, w2k_ref)
    ms = jnp.sum(kc * kc, axis=-1, keepdims=True) * (1.0 / HEAD_DIM)
    kc = kc * lax.rsqrt(ms + EPS) * knw_ref[...]
    kc_ref[0, 0] = _rope(kc, rc_ref[...], ra_ref[...], rb_ref[...]).astype(BF16)
    vc_ref[0, 0] = mlp(xv_ref, pv_ref, w1v_ref, w2v_ref).astype(BF16)


def _compress(xk, xv, pk, pv, w1k, w1v, w2k, w2v, knw, rc, ra, rb):
    B, G, C, W = xk.shape
    blk = pl.BlockSpec((1, 1, C, W), lambda b, g: (b, g, 0, 0))
    oblk = pl.BlockSpec((1, 1, C, LANES), lambda b, g: (b, g, 0, 0))
    return pl.pallas_call(
        _compress_kernel,
        grid=(B, G),
        in_specs=[blk, blk, _const_spec(pk.shape), _const_spec(pv.shape),
                  _const_spec(w1k.shape), _const_spec(w1v.shape), _const_spec(w2k.shape), _const_spec(w2v.shape),
                  _const_spec(knw.shape), _const_spec(rc.shape), _const_spec(ra.shape), _const_spec(rb.shape)],
        out_specs=[oblk, oblk],
        out_shape=[jax.ShapeDtypeStruct((B, G, C, LANES), BF16)] * 2,
        compiler_params=pltpu.CompilerParams(
            dimension_semantics=("parallel", "parallel"), vmem_limit_bytes=VMEM_LIMIT_BYTES),
        name="compress",
    )(xk, xv, pk, pv, w1k, w1v, w2k, w2v, knw, rc, ra, rb)


def _rglru_kernel(xr_ref, gy_ref, cw_ref, cb_ref, wax_ref, bax_ref, lam_ref, o_ref,
                  xs_ref, hs_ref, tail_ref, h_ref):
    nb, ts, w = xr_ref.shape
    halo = (CONV_W - 1) * nb

    @pl.when(pl.program_id(0) == 0)
    def _():
        tail_ref[...] = jnp.zeros_like(tail_ref)
        h_ref[...] = jnp.zeros_like(h_ref)

    n_lt = w // LANES
    lanes = lambda a, c: a[:, c * LANES:(c + 1) * LANES]
    x_bm = xr_ref[...].reshape(nb * ts, w).astype(F32)
    for c in range(n_lt):
        xs_ref[c] = lanes(x_bm, c)
    x = jnp.concatenate(
        [jnp.concatenate([xs_ref[c, pl.ds(t, nb, stride=ts), :] for c in range(n_lt)], axis=1)
         for t in range(ts)], axis=0)
    xe = jnp.concatenate([tail_ref[...], x], axis=0)
    tail_ref[...] = x[ts * nb - halo:, :]
    xc = cb_ref[...] + x * cw_ref[CONV_W - 1:CONV_W, :]
    for j in range(CONV_W - 1):
        xc = xc + xe[j * nb:j * nb + ts * nb, :] * cw_ref[j:j + 1, :]

    xcb = xc.astype(BF16)
    pre = [jnp.dot(xcb[:, n * LRU_BD:(n + 1) * LRU_BD], wax_ref[n], preferred_element_type=F32)
           for n in range(LRU_BLOCKS)]
    tr = jnp.tanh(jnp.concatenate([p[:, :LRU_BD] for p in pre], axis=1) + bax_ref[0:1, :])
    ti = jnp.tanh(jnp.concatenate([p[:, LRU_BD:] for p in pre], axis=1) + bax_ref[1:2, :])

    half_c = (-0.5 * LRU_C) * jax.nn.softplus(-lam_ref[...])
    log_a = half_c * tr + half_c
    a = jnp.exp(log_a)
    z = -jnp.tanh(log_a) * (1.0 + a * a)
    root = jnp.where(z > 0.0, z * lax.rsqrt(z), 0.0)
    b = root * ((0.5 * ti + 0.5) * xc)

    h = h_ref[...]
    for t in range(ts):
        h = a[t * nb:(t + 1) * nb, :] * h + b[t * nb:(t + 1) * nb, :]
        for c in range(n_lt):
            hs_ref[c, pl.ds(t, nb, stride=ts), :] = lanes(h, c)
    h_ref[...] = h
    h_bm = jnp.concatenate([hs_ref[c] for c in range(n_lt)], axis=1)
    o_ref[...] = (h_bm.reshape(nb, ts, w) * gy_ref[...].astype(F32)).astype(BF16)


def _rglru(xr, gy, cw, cb, wax, bax, lam):
    B, S, W = xr.shape
    ts = TS_LRU
    assert S % ts == 0
    tok = pl.BlockSpec((B, ts, W), lambda s: (0, s, 0))
    return pl.pallas_call(
        _rglru_kernel,
        grid=(S // ts,),
        in_specs=[tok, tok, _const_spec(cw.shape), _const_spec(cb.shape), _const_spec(wax.shape),
                  _const_spec(bax.shape), _const_spec(lam.shape)],
        out_specs=tok,
        out_shape=jax.ShapeDtypeStruct((B, S, W), BF16),
        scratch_shapes=[pltpu.VMEM((W // LANES, B * ts, LANES), F32), pltpu.VMEM((W // LANES, B * ts, LANES), F32),
                        pltpu.VMEM(((CONV_W - 1) * B, W), F32), pltpu.VMEM((B, W), F32)],
        compiler_params=pltpu.CompilerParams(
            dimension_semantics=("arbitrary",), vmem_limit_bytes=VMEM_LIMIT_BYTES),
        name="rglru",
    )(xr, gy, cw, cb, wax, bax, lam)


_NT = (((1,), (1,)), ((), ()))


def _select_bias(imp, cur, n_sel):
    n_slc, nqb = imp.shape
    blk = lax.broadcasted_iota(jnp.int32, (n_slc, 1), 0)
    causal = blk <= cur
    forced = (blk == 0) | (blk == cur) | (blk == cur - 1)
    score = jnp.where(causal, jnp.where(forced, FORCE_SCORE, imp), NEG)
    n_ch = n_slc // SUBLANES
    chunks = [score[c * SUBLANES:(c + 1) * SUBLANES] for c in range(n_ch)]
    ranks = [jnp.zeros((SUBLANES, nqb), F32) for _ in range(n_ch)]
    sub = lax.broadcasted_iota(jnp.int32, (SUBLANES, 1), 0)
    for k in range(n_slc):
        sk = jnp.broadcast_to(score[k:k + 1, :], (SUBLANES, nqb))
        k_chunk, k_row = divmod(k, SUBLANES)
        for c in range(n_ch):
            if c > k_chunk:
                one = jnp.where(sk >= chunks[c], 1.0, 0.0)
            elif c < k_chunk:
                one = jnp.where(sk > chunks[c], 1.0, 0.0)
            else:
                one = jnp.where(sub > k_row, jnp.where(sk >= chunks[c], 1.0, 0.0),
                                jnp.where(sk > chunks[c], 1.0, 0.0))
            ranks[c] = ranks[c] + one
    rank = jnp.concatenate(ranks, axis=0)
    return jnp.where(causal & (rank < float(n_sel)), 0.0, NEG)


def _nsa_kernel(q_ref, kc_ref, vo_ref, ks_ref, vst_ref, kw_ref, vwt_ref, g_ref, o_ref, bp_ref, qa_ref, s_ref):
    i = pl.program_id(1)
    n_grp = q_ref.shape[1]
    nq = HPG * Q_BLOCK
    lane = lax.broadcasted_iota(jnp.int32, (1, nq), 1)
    tl = lane & (Q_BLOCK - 1)
    t = i * Q_BLOCK + tl
    tq = i * Q_BLOCK + lax.broadcasted_iota(jnp.int32, (1, Q_BLOCK), 1)
    cur = lax.shift_right_logical(tq, SLC_BLOCK.bit_length() - 1)
    n_cmp_pad = kc_ref.shape[2]
    n_slc = vo_ref.shape[2] - HEAD_DIM - ONES_ROWS
    cmp_end = lax.broadcasted_iota(jnp.int32, (n_cmp_pad, 1), 0) * CMP_STRIDE + (CMP_BLOCK - 1)
    lane_q = lax.broadcasted_iota(jnp.int32, (nq, LANES), 1)
    row_q = lax.broadcasted_iota(jnp.int32, (Q_BLOCK, 1), 0)
    wk = WINDOW + Q_BLOCK
    w_off = pl.multiple_of(i * Q_BLOCK, Q_BLOCK)

    groups = range(n_grp)
    qs = [q_ref[0, g, 0] for g in groups]
    dot_nt = lambda a, b: lax.dot_general(a, b, _NT, preferred_element_type=F32)
    dot_nn = lambda a, b: jnp.dot(a, b, preferred_element_type=F32)


    sc = [dot_nt(kc_ref[0, g], qs[g]) for g in groups]
    sw = [dot_nt(kw_ref[0, g, pl.ds(w_off, wk), :], qs[g]) for g in groups]
    pc = []
    for g in groups:
        s = jnp.where(cmp_end <= t, sc[g], NEG)
        pc.append(jnp.exp2(s - jnp.max(s, axis=0, keepdims=True)).astype(BF16))
    n_val = HEAD_DIM + n_slc
    rc = []
    for g in groups:
        r = dot_nn(vo_ref[0, g], pc[g])
        r = r[:n_val] / r[n_val:n_val + 1]
        rc.append(jnp.where(t >= CMP_BLOCK - 1, r, 0.0))
    pw = []
    for g in groups:
        s = sw[g]
        s = jnp.concatenate([jnp.where(row_q > tl, s[:Q_BLOCK], NEG), s[Q_BLOCK:WINDOW],
                             jnp.where(row_q <= tl, s[WINDOW:], NEG)], axis=0)
        pw.append(jnp.exp2(s - jnp.max(s, axis=0, keepdims=True)).astype(BF16))
    o_win = []
    for g in groups:
        r = dot_nn(vwt_ref[0, g, :, pl.ds(w_off, wk)], pw[g])
        o_win.append(r[:HEAD_DIM] / r[HEAD_DIM:HEAD_DIM + 1])

    for g in groups:
        imp_h = rc[g][HEAD_DIM:]
        imp = imp_h[:, 0:Q_BLOCK]
        for h in range(1, HPG):
            imp = imp + imp_h[:, h * Q_BLOCK:(h + 1) * Q_BLOCK]
        bp_ref[g] = jnp.zeros(bp_ref.shape[1:], F32)
        bp_ref[g, HEAD_DIM:HEAD_DIM + n_slc, :] = _select_bias(imp, cur, min(N_SLC, n_slc))
        bt = bp_ref[g].T.astype(BF16)
        qa_ref[g] = jnp.where(lane_q < HEAD_DIM, qs[g], jnp.concatenate([bt] * HPG, axis=0))

    n_full = lax.shift_right_logical(i * Q_BLOCK, TK_SLC.bit_length() - 1)
    n_pairs = lax.shift_right_logical(n_full, 1) + 1

    def key_off(u):
        tile = jnp.where(u == 0, n_full, jnp.where(u <= n_full, u - 1, u))
        return pl.multiple_of(tile * TK_SLC, TK_SLC)

    def scores(u, slot, masked):
        off = key_off(u)
        for g in groups:
            s = dot_nt(ks_ref[0, g, pl.ds(off, TK_SLC), :], qa_ref[g])
            if masked:
                s = jnp.where(off + lax.broadcasted_iota(jnp.int32, (TK_SLC, 1), 0) <= t, s, NEG)
            s_ref[slot, g] = s

    def consume(u, slot, carry):
        off = key_off(u)
        ps, stats = [], []
        for g in groups:
            m, acc = carry[g]
            s = s_ref[slot, g]
            m_new = jnp.maximum(m, jnp.max(s, axis=0, keepdims=True))
            stats.append((m_new, jnp.exp2(m - m_new) * acc))
            ps.append(jnp.exp2(s - m_new).astype(BF16))
        pv = [dot_nn(vst_ref[0, g, :, pl.ds(off, TK_SLC)], ps[g]) for g in groups]
        return tuple((stats[g][0], stats[g][1] + pv[g]) for g in groups)

    def pair_step(j, carry):
        scores(2 * j + 1, 1, False)
        carry = consume(2 * j, 0, carry)
        scores(2 * j + 2, 0, False)
        return consume(2 * j + 1, 1, carry)

    init = (jnp.full((1, nq), NEG, F32), jnp.zeros((vst_ref.shape[2], nq), F32))
    scores(0, 0, True)
    carry = lax.fori_loop(0, n_pairs - 1, pair_step, (init,) * n_grp)
    last = 2 * n_pairs - 1
    scores(last, 1, False)
    carry = consume(last - 1, 0, carry)
    carry = consume(last, 1, carry)

    gt = g_ref[0]
    for g in groups:
        _, acc_s = carry[g]
        o_slc = acc_s[:HEAD_DIM] / acc_s[HEAD_DIM:HEAD_DIM + 1]
        branches = (rc[g][:HEAD_DIM], o_slc, o_win[g])
        heads = []
        for h in range(HPG):
            cols = slice(h * Q_BLOCK, (h + 1) * Q_BLOCK)
            o_h = 0.0
            for br in range(N_BRANCH):
                r = (br * n_grp + g) * HPG + h
                o_h = o_h + gt[r:r + 1, :] * branches[br][:, cols]
            heads.append(o_h)
        for pr in range(HPG // 2):
            pair = jnp.concatenate(heads[2 * pr:2 * pr + 2], axis=0)
            c0 = (g * HPG + 2 * pr) * HEAD_DIM
            o_ref[0, :, c0:c0 + 2 * HEAD_DIM] = pair.T.astype(BF16)


def _nsa(q, kc, vo, ks, vst, kw, vwt, g):
    B, G, nqb, nq, _ = q.shape
    S = ks.shape[2]
    n_slc = S // SLC_BLOCK
    assert HEAD_DIM + n_slc <= LANES and (S // TK_SLC) % 2 == 0
    per_q = lambda shape: pl.BlockSpec((1, G, 1) + shape, lambda b, i: (b, 0, i, 0, 0))
    per_b = lambda a: pl.BlockSpec((1,) + a.shape[1:], lambda b, i: (b, 0, 0, 0))
    return pl.pallas_call(
        _nsa_kernel,
        grid=(B, nqb),
        in_specs=[per_q((nq, LANES)), per_b(kc), per_b(vo), per_b(ks), per_b(vst), per_b(kw), per_b(vwt),
                  pl.BlockSpec((1, g.shape[1], Q_BLOCK), lambda b, i: (b, 0, i))],
        out_specs=pl.BlockSpec((1, Q_BLOCK, NSA_WIDTH), lambda b, i: (b, i, 0)),
        out_shape=jax.ShapeDtypeStruct((B, S, NSA_WIDTH), BF16),
        scratch_shapes=[pltpu.VMEM((G, LANES, Q_BLOCK), F32), pltpu.VMEM((G, nq, LANES), BF16),
                        pltpu.VMEM((2, G, TK_SLC, nq), F32)],
        compiler_params=pltpu.CompilerParams(
            dimension_semantics=("parallel", "arbitrary"), vmem_limit_bytes=VMEM_LIMIT_BYTES),
        name="nsa",
    )(q, kc, vo, ks, vst, kw, vwt, g)


def _out_kernel(x_ref, on_ref, ol_ref, gm_ref, wa_ref, wb_ref, wo_ref, n2_ref, w1_ref, w2_ref, o_ref):
    ua = jnp.dot(on_ref[...], wa_ref[...], preferred_element_type=F32)
    ub = jnp.dot(ol_ref[...], wb_ref[...], preferred_element_type=F32)
    gm = gm_ref[...]
    merged = gm[:, :D_MODEL].astype(F32) * ua + gm[:, D_MODEL:].astype(F32) * ub
    h = x_ref[...] + jnp.dot(merged.astype(BF16), wo_ref[...], preferred_element_type=F32)
    ms = jnp.mean(h * h, axis=-1, keepdims=True)
    hn = (h * lax.rsqrt(ms + EPS) * n2_ref[...]).astype(BF16)
    acc = h
    for c in range(D_FF // FF_CHUNK):
        u = jnp.dot(hn, w1_ref[:, c * FF_CHUNK:(c + 1) * FF_CHUNK], preferred_element_type=F32)
        u = jnp.square(jnp.maximum(u, 0.0)).astype(BF16)
        acc = acc + jnp.dot(u, w2_ref[c * FF_CHUNK:(c + 1) * FF_CHUNK, :], preferred_element_type=F32)
    o_ref[...] = acc


def _out(x2, on, ol, gm, wa, wb, wo, n2, w1, w2):
    T, D = x2.shape
    tm = TM_OUT
    tok = lambda width: pl.BlockSpec((tm, width), lambda i: (i, 0))
    return pl.pallas_call(
        _out_kernel,
        grid=(T // tm,),
        in_specs=[tok(D), tok(NSA_WIDTH), tok(LRU_WIDTH), tok(2 * D_MODEL),
                  _const_spec(wa.shape), _const_spec(wb.shape), _const_spec(wo.shape), _const_spec(n2.shape),
                  _const_spec(w1.shape), _const_spec(w2.shape)],
        out_specs=tok(D),
        out_shape=jax.ShapeDtypeStruct((T, D), F32),
        compiler_params=pltpu.CompilerParams(
            dimension_semantics=("parallel",), vmem_limit_bytes=VMEM_LIMIT_BYTES),
        name="out",
    )(x2, on, ol, gm, wa, wb, wo, n2, w1, w2)


def _rope_tables(pos):
    inv = ROPE_THETA ** (-jnp.arange(ROPE_HALF, dtype=F32) / ROPE_HALF)
    ang = pos.astype(F32)[:, None] * inv[None, :]
    cos, sin = jnp.cos(ang), jnp.sin(ang)
    n = pos.shape[0]
    rest = HEAD_DIM - ROPE_DIM
    c = jnp.concatenate([cos, cos, jnp.ones((n, rest), F32)], axis=1)
    sa = jnp.concatenate([-sin, jnp.zeros((n, HEAD_DIM - ROPE_HALF), F32)], axis=1)
    sb = jnp.concatenate([jnp.zeros((n, ROPE_HALF), F32), sin, jnp.zeros((n, rest), F32)], axis=1)
    two = lambda a: jnp.concatenate([a] * (LANES // HEAD_DIM), axis=1)
    return two(c), two(sa), two(sb)


def _layer(h, norm1_w, w_in, q_norm_w, k_norm_w, phi_k_pos, phi_k_w1, phi_k_w2, phi_v_pos, phi_v_w1, phi_v_w2,
           conv_w, conv_b, lru_wa, lru_ba, lru_wx, lru_bx, lru_lambda, w_nsa_up, w_lru_up, w_o, norm2_w,
           w_ff1, w_ff2):
    B, S, D = h.shape
    assert D == D_MODEL and S % TK_SLC == 0 and S % TM_IN == 0 and (B * S) % TM_OUT == 0
    nqb = S // Q_BLOCK
    n_chunks = S // CMP_STRIDE
    n_cmp = (S - CMP_BLOCK) // CMP_STRIDE + 1
    n_slc = S // SLC_BLOCK

    w_pad = jnp.concatenate(
        [w_in[:, :G_REAL_END], jnp.zeros((D, G_PAD - N_BRANCH * N_HEADS), w_in.dtype), w_in[:, G_REAL_END:]],
        axis=1)
    col = jnp.arange(IN_COLS_PAD)
    is_gate = ((col >= OFF_G) & (col < OFF_XR)) | (col >= OFF_GM)
    w_pad = (w_pad * jnp.where(is_gate, 0.5, 1.0)).astype(BF16)
    w_t = jnp.concatenate([w_pad[:, OFF_VS:OFF_VS + KV_WIDTH], w_pad[:, OFF_VW:OFF_VW + KV_WIDTH],
                           w_pad[:, OFF_G:OFF_G + GATE_ROWS]], axis=1).T
    qnw = jnp.tile(q_norm_w, N_HEADS)[None, :]
    ksw = jnp.tile(k_norm_w[1], N_KV)[None, :]
    kww = jnp.tile(k_norm_w[2], N_KV)[None, :]
    rc, ra, rb = _rope_tables(jnp.arange(S))
    seg = jnp.arange(NSA_WIDTH) // HEAD_DIM
    bd = (seg[:, None] == seg[None, :]).astype(BF16)

    q5, kcv, ks_aug, vst_aug, kw_aug, vwt_aug, gt, xr, gy, gm = _in_proj(
        h, norm1_w[None, :], w_pad, w_t, qnw, ksw, kww, rc, ra, rb, bd)

    def chunks(raw):
        return raw.reshape(B, S, N_KV, HEAD_DIM).transpose(0, 2, 1, 3).reshape(
            B, N_KV, n_chunks, CMP_STRIDE * HEAD_DIM)

    pos2 = lambda p: p.reshape(2, CMP_STRIDE * HEAD_DIM)
    pad_w2 = lambda w: jnp.concatenate([w, jnp.zeros((PHI_HIDDEN, LANES - HEAD_DIM), w.dtype)], axis=1).astype(BF16)
    knw0 = jnp.concatenate([k_norm_w[0], jnp.zeros((LANES - HEAD_DIM,), F32)])[None, :]
    cmp_end = jnp.arange(n_chunks) * CMP_STRIDE + (CMP_BLOCK - 1)
    cc, ca, cb_ = _rope_tables(cmp_end)
    kc, vc = _compress(chunks(kcv[..., :KV_WIDTH]), chunks(kcv[..., KV_WIDTH:]),
                       pos2(phi_k_pos), pos2(phi_v_pos), phi_k_w1.astype(BF16), phi_v_w1.astype(BF16),
                       pad_w2(phi_k_w2), pad_w2(phi_v_w2), knw0, cc, ca, cb_)
    cmp_lo = jnp.arange(n_chunks) * CMP_STRIDE
    slc_lo = jnp.arange(n_slc) * SLC_BLOCK
    overlap_t = ((cmp_lo[None, :] <= slc_lo[:, None] + SLC_BLOCK - 1)
                 & (cmp_lo[None, :] + CMP_BLOCK - 1 >= slc_lo[:, None])
                 & (jnp.arange(n_chunks)[None, :] < n_cmp)).astype(BF16)
    ones_rows = lambda n: jnp.ones((B, N_KV, ONES_ROWS, n), BF16)
    vo = jnp.concatenate(
        [vc[..., :HEAD_DIM].transpose(0, 1, 3, 2), jnp.broadcast_to(overlap_t, (B, N_KV) + overlap_t.shape),
         ones_rows(n_chunks)], axis=2)

    wax = (0.5 * jnp.concatenate([lru_wa, lru_wx], axis=2)).astype(BF16)
    bax = 0.5 * jnp.stack([lru_ba.reshape(LRU_WIDTH), lru_bx.reshape(LRU_WIDTH)])
    o_lru = _rglru(xr, gy, conv_w, conv_b[None, :], wax, bax, lru_lambda[None, :])

    o_nsa = _nsa(q5, kc, vo, ks_aug, vst_aug, kw_aug, vwt_aug, gt).reshape(B * S, NSA_WIDTH)

    out = _out(h.reshape(B * S, D), o_nsa, o_lru.reshape(B * S, LRU_WIDTH), gm.reshape(B * S, 2 * D_MODEL),
               w_nsa_up.astype(BF16), w_lru_up.astype(BF16), w_o.astype(BF16), norm2_w[None, :],
               w_ff1.astype(BF16), w_ff2.astype(BF16))
    return out.reshape(B, S, D)


def kernel(x, norm1_w, w_in, q_norm_w, k_norm_w, phi_k_pos, phi_k_w1, phi_k_w2, phi_v_pos, phi_v_w1, phi_v_w2, conv_w, conv_b, lru_wa, lru_ba, lru_wx, lru_bx, lru_lambda, w_nsa_up, w_lru_up, w_o, norm2_w, w_ff1, w_ff2):
    params = (norm1_w, w_in, q_norm_w, k_norm_w, phi_k_pos, phi_k_w1, phi_k_w2, phi_v_pos, phi_v_w1, phi_v_w2,
              conv_w, conv_b, lru_wa, lru_ba, lru_wx, lru_bx, lru_lambda, w_nsa_up, w_lru_up, w_o, norm2_w,
              w_ff1, w_ff2)
    h = x
    for layer in range(w_in.shape[0]):
        h = _layer(h, *(p[layer] for p in params))
    return h
```

```python
import functools

import jax
import jax.numpy as jnp
from jax import lax
from jax.experimental import pallas as pl
from jax.experimental.pallas import tpu as pltpu

F32 = jnp.float32
BF16 = jnp.bfloat16

D_MODEL = 1024
N_HEADS = 8
N_KV = 2
HEAD_DIM = 64
HPG = N_HEADS // N_KV
NSA_WIDTH = N_HEADS * HEAD_DIM
KV_WIDTH = N_KV * HEAD_DIM
ROPE_DIM = HEAD_DIM // 4
ROPE_HALF = ROPE_DIM // 2
ROPE_THETA = 500000.0
CMP_BLOCK = 32
CMP_STRIDE = 16
SLC_BLOCK = 64
N_SLC = 16
WINDOW = 512
Q_BLOCK = 128
PHI_HIDDEN = 256
N_BRANCH = 3
LRU_WIDTH = D_MODEL
LRU_BLOCKS = 8
LRU_BD = LRU_WIDTH // LRU_BLOCKS
LRU_C = 8.0
CONV_W = 4
D_FF = 4 * D_MODEL
EPS = 1e-6
NEG = -1e30
FORCE_SCORE = 1e4
LOG2E = 1.4426950408889634

LANES = 128
SUBLANES = 8
VMEM_LIMIT_BYTES = 56 * 1024 * 1024

ONES_ROWS = 16
G_PAD = LANES
OFF_Q = 0
OFF_KC = OFF_Q + NSA_WIDTH
OFF_VC = OFF_KC + KV_WIDTH
OFF_KS = OFF_VC + KV_WIDTH
OFF_VS = OFF_KS + KV_WIDTH
OFF_KW = OFF_VS + KV_WIDTH
OFF_VW = OFF_KW + KV_WIDTH
OFF_G = OFF_VW + KV_WIDTH
OFF_XR = OFF_G + G_PAD
OFF_YR = OFF_XR + LRU_WIDTH
OFF_GM = OFF_YR + LRU_WIDTH
IN_COLS_PAD = OFF_GM + 2 * D_MODEL
G_REAL_END = NSA_WIDTH + 6 * KV_WIDTH + N_BRANCH * N_HEADS

TM_IN = 512
TS_LRU = 64
TM_OUT = 512
TK_SLC = 512
FF_CHUNK = 1024


def _const_spec(shape):
    nd = len(shape)
    return pl.BlockSpec(shape, lambda *_: (0,) * nd, pipeline_mode=pl.Buffered(1))


def _head_rmsnorm(v, w_tiled, bd):
    sq = v * v
    hi = sq.astype(BF16)
    lo = (sq - hi.astype(F32)).astype(BF16)
    ss = jnp.dot(hi, bd, preferred_element_type=F32) + jnp.dot(lo, bd, preferred_element_type=F32)
    return v * lax.rsqrt(ss * (1.0 / HEAD_DIM) + EPS) * w_tiled


def _rope(v, c, sa, sb):
    w = v.shape[-1]
    reps = w // LANES
    if reps > 1:
        c = jnp.concatenate([c] * reps, axis=1)
        sa = jnp.concatenate([sa] * reps, axis=1)
        sb = jnp.concatenate([sb] * reps, axis=1)
    up = pltpu.roll(v, w - ROPE_HALF, 1)
    dn = pltpu.roll(v, ROPE_HALF, 1)
    return v * c + up * sa + dn * sb


def _inproj_kernel(x_ref, n1_ref, w_ref, wt_ref, qnw_ref, ksw_ref, kww_ref, rc_ref, ra_ref, rb_ref, bd_ref,
                   kw_base_ref, vw_base_ref,
                   q_ref, kcv_ref, ks_ref, vs_ref, kw_ref, vw_ref, g_ref, xr_ref, gy_ref, gm_ref):
    del kw_base_ref, vw_base_ref
    tm = x_ref.shape[1]
    n_grp = ks_ref.shape[1]
    x = x_ref[0]
    ms = jnp.mean(x * x, axis=-1, keepdims=True)
    xn = (x * lax.rsqrt(ms + EPS) * n1_ref[...]).astype(BF16)

    def proj(off, width):
        return jnp.dot(xn, w_ref[:, off:off + width], preferred_element_type=F32)

    rc, ra, rb = rc_ref[...], ra_ref[...], rb_ref[...]
    bd = bd_ref[...]
    bd_kv = bd[:KV_WIDTH, :KV_WIDTH]
    lane = lax.broadcasted_iota(jnp.int32, (tm, LANES), 1)
    low = lane < HEAD_DIM

    def head_tile(v, head):
        tile = v[:, (head // 2) * LANES:(head // 2 + 1) * LANES]
        return pltpu.roll(tile, HEAD_DIM, 1) if head % 2 else tile

    q = _rope(_head_rmsnorm(proj(OFF_Q, NSA_WIDTH), qnw_ref[...], bd), rc, ra, rb)
    q = q * (HEAD_DIM ** -0.5 * LOG2E)
    q_pad = jnp.where(lane == HEAD_DIM, NEG, 0.0)
    for head in range(N_HEADS):
        g, h = divmod(head, HPG)
        v = jnp.where(low, head_tile(q, head), q_pad).astype(BF16)
        for qb in range(tm // Q_BLOCK):
            q_ref[0, g, qb, h * Q_BLOCK:(h + 1) * Q_BLOCK, :] = v[qb * Q_BLOCK:(qb + 1) * Q_BLOCK]

    kcv_ref[0] = proj(OFF_KC, 2 * KV_WIDTH).astype(BF16)

    ks = _rope(_head_rmsnorm(proj(OFF_KS, KV_WIDTH), ksw_ref[...], bd_kv), rc, ra, rb)
    pos = pl.program_id(1) * tm + lax.broadcasted_iota(jnp.int32, (tm, 1), 0)
    block = lax.shift_right_logical(pos, SLC_BLOCK.bit_length() - 1)
    onehot = jnp.where(lane - HEAD_DIM == block, 1.0, 0.0)
    kw = _rope(_head_rmsnorm(proj(OFF_KW, KV_WIDTH), kww_ref[...], bd_kv), rc, ra, rb)
    for g in range(n_grp):
        ks_ref[0, g] = jnp.where(low, head_tile(ks, g), onehot).astype(BF16)
        kw_ref[0, g] = jnp.where(low, head_tile(kw, g), 0.0).astype(BF16)

    t_out = lax.dot_general(wt_ref[...], xn, _NT, preferred_element_type=F32)
    ones = jnp.ones((ONES_ROWS, tm), BF16)
    for g in range(n_grp):
        rows = slice(g * HEAD_DIM, (g + 1) * HEAD_DIM)
        vs_ref[0, g] = jnp.concatenate([t_out[:KV_WIDTH][rows].astype(BF16), ones], axis=0)
        vw_ref[0, g] = jnp.concatenate([t_out[KV_WIDTH:2 * KV_WIDTH][rows].astype(BF16), ones], axis=0)
    g_ref[0] = 0.5 * jnp.tanh(t_out[2 * KV_WIDTH:]) + 0.5
    xr_ref[0] = proj(OFF_XR, LRU_WIDTH).astype(BF16)
    gy_ref[0] = jax.nn.gelu(proj(OFF_YR, LRU_WIDTH)).astype(BF16)
    gm_ref[0] = (0.5 * jnp.tanh(proj(OFF_GM, 2 * D_MODEL)) + 0.5).astype(BF16)


GATE_ROWS = 32


def _in_proj(x, n1, w_pad, w_t, qnw, ksw, kww, rc, ra, rb, bd):
    B, S, D = x.shape
    tm = TM_IN
    G = N_KV
    assert WINDOW % tm == 0 and tm % Q_BLOCK == 0
    pad_blocks = WINDOW // tm
    v_rows = HEAD_DIM + ONES_ROWS
    kw_base = jnp.broadcast_to((jnp.arange(LANES) == HEAD_DIM).astype(BF16), (B, G, WINDOW + S, LANES))
    vw_base = jnp.zeros((B, G, v_rows, WINDOW + S), BF16)

    tok = lambda width: pl.BlockSpec((1, tm, width), lambda b, s: (b, s, 0))
    tab = pl.BlockSpec((tm, LANES), lambda b, s: (s, 0))
    any_spec = pl.BlockSpec(memory_space=pl.ANY)
    out_specs = [
        pl.BlockSpec((1, G, tm // Q_BLOCK, HPG * Q_BLOCK, LANES), lambda b, s: (b, 0, s, 0, 0)),
        tok(2 * KV_WIDTH),
        pl.BlockSpec((1, G, tm, LANES), lambda b, s: (b, 0, s, 0)),
        pl.BlockSpec((1, G, v_rows, tm), lambda b, s: (b, 0, 0, s)),
        pl.BlockSpec((1, G, tm, LANES), lambda b, s: (b, 0, s + pad_blocks, 0)),
        pl.BlockSpec((1, G, v_rows, tm), lambda b, s: (b, 0, 0, s + pad_blocks)),
        pl.BlockSpec((1, GATE_ROWS, tm), lambda b, s: (b, 0, s)),
        tok(LRU_WIDTH), tok(LRU_WIDTH), tok(2 * D_MODEL)]
    out_shape = [
        jax.ShapeDtypeStruct((B, G, S // Q_BLOCK, HPG * Q_BLOCK, LANES), BF16),
        jax.ShapeDtypeStruct((B, S, 2 * KV_WIDTH), BF16),
        jax.ShapeDtypeStruct((B, G, S, LANES), BF16),
        jax.ShapeDtypeStruct((B, G, v_rows, S), BF16),
        jax.ShapeDtypeStruct(kw_base.shape, BF16),
        jax.ShapeDtypeStruct(vw_base.shape, BF16),
        jax.ShapeDtypeStruct((B, GATE_ROWS, S), F32),
        jax.ShapeDtypeStruct((B, S, LRU_WIDTH), BF16),
        jax.ShapeDtypeStruct((B, S, LRU_WIDTH), BF16),
        jax.ShapeDtypeStruct((B, S, 2 * D_MODEL), BF16)]
    return pl.pallas_call(
        _inproj_kernel,
        grid=(B, S // tm),
        in_specs=[tok(D), _const_spec((1, D)), _const_spec((D, IN_COLS_PAD)), _const_spec(w_t.shape),
                  _const_spec((1, NSA_WIDTH)), _const_spec((1, KV_WIDTH)), _const_spec((1, KV_WIDTH)),
                  tab, tab, tab, _const_spec((NSA_WIDTH, NSA_WIDTH)), any_spec, any_spec],
        out_specs=out_specs,
        out_shape=out_shape,
        input_output_aliases={11: 4, 12: 5},
        compiler_params=pltpu.CompilerParams(
            dimension_semantics=("parallel", "parallel"), vmem_limit_bytes=VMEM_LIMIT_BYTES),
        name="in_proj",
    )(x, n1, w_pad, w_t, qnw, ksw, kww, rc, ra, rb, bd, kw_base, vw_base)


def _compress_kernel(xk_ref, xv_ref, pk_ref, pv_ref, w1k_ref, w1v_ref, w2k_ref, w2v_ref,
                     knw_ref, rc_ref, ra_ref, rb_ref, kc_ref, vc_ref):
    half = CMP_STRIDE * HEAD_DIM

    def mlp(x_ref, p_ref, w1_ref, w2_ref):
        x = x_ref[0, 0].astype(F32)
        lo = (x + p_ref[0:1, :]).astype(BF16)
        hi = (x + p_ref[1:2, :]).astype(BF16)
        a = jnp.dot(lo, w1_ref[:half, :], preferred_element_type=F32)
        b = jnp.dot(hi, w1_ref[half:, :], preferred_element_type=F32)
        hid = a + pltpu.roll(b, b.shape[0] - 1, 0)
        return jnp.dot(jax.nn.gelu(hid).astype(BF16), w2_ref[...], preferred_element_type=F32)

    kc = mlp(xk_ref, pk_ref, w1k_ref, w2k_ref)
    ms = jnp.sum(kc * kc, axis=-1, keepdims=True) * (1.0 / HEAD_DIM)
    kc = kc * lax.rsqrt(ms + EPS) * knw_ref[...]
    kc_ref[0, 0] = _rope(kc, rc_ref[...], ra_ref[...], rb_ref[...]).astype(BF16)
    vc_ref[0, 0] = mlp(xv_ref, pv_ref, w1v_ref, w2v_ref).astype(BF16)


def _compress(xk, xv, pk, pv, w1k, w1v, w2k, w2v, knw, rc, ra, rb):
    B, G, C, W = xk.shape
    blk = pl.BlockSpec((1, 1, C, W), lambda b, g: (b, g, 0, 0))
    oblk = pl.BlockSpec((1, 1, C, LANES), lambda b, g: (b, g, 0, 0))
    return pl.pallas_call(
        _compress_kernel,
        grid=(B, G),
        in_specs=[blk, blk, _const_spec(pk.shape), _const_spec(pv.shape),
                  _const_spec(w1k.shape), _const_spec(w1v.shape), _const_spec(w2k.shape), _const_spec(w2v.shape),
                  _const_spec(knw.shape), _const_spec(rc.shape), _const_spec(ra.shape), _const_spec(rb.shape)],
        out_specs=[oblk, oblk],
        out_shape=[jax.ShapeDtypeStruct((B, G, C, LANES), BF16)] * 2,
        compiler_params=pltpu.CompilerParams(
            dimension_semantics=("parallel", "parallel"), vmem_limit_bytes=VMEM_LIMIT_BYTES),
        name="compress",
    )(xk, xv, pk, pv, w1k, w1v, w2k, w2v, knw, rc, ra, rb)


def _rglru_kernel(xr_ref, gy_ref, perm_ref, cw_ref, cb_ref, wax_ref, bax_ref, lam_ref, o_ref,
                  hs_ref, tail_ref, h_ref):
    nb, ts, w = xr_ref.shape
    halo = (CONV_W - 1) * nb

    @pl.when(pl.program_id(0) == 0)
    def _():
        tail_ref[...] = jnp.zeros_like(tail_ref)
        h_ref[...] = jnp.zeros_like(h_ref)

    x = jnp.dot(perm_ref[...], xr_ref[...].reshape(nb * ts, w), preferred_element_type=F32)
    n_lt = w // LANES
    lanes = lambda a, c: a[:, c * LANES:(c + 1) * LANES]
    xe = jnp.concatenate([tail_ref[...], x], axis=0)
    tail_ref[...] = x[ts * nb - halo:, :]
    xc = cb_ref[...] + x * cw_ref[CONV_W - 1:CONV_W, :]
    for j in range(CONV_W - 1):
        xc = xc + xe[j * nb:j * nb + ts * nb, :] * cw_ref[j:j + 1, :]

    xcb = xc.astype(BF16)
    pre = [jnp.dot(xcb[:, n * LRU_BD:(n + 1) * LRU_BD], wax_ref[n], preferred_element_type=F32)
           for n in range(LRU_BLOCKS)]
    tr = jnp.tanh(jnp.concatenate([p[:, :LRU_BD] for p in pre], axis=1) + bax_ref[0:1, :])
    ti = jnp.tanh(jnp.concatenate([p[:, LRU_BD:] for p in pre], axis=1) + bax_ref[1:2, :])

    half_c = (-0.5 * LRU_C) * jax.nn.softplus(-lam_ref[...])
    log_a = half_c * tr + half_c
    a = jnp.exp(log_a)
    z = -jnp.tanh(log_a) * (1.0 + a * a)
    root = jnp.where(z > 0.0, z * lax.rsqrt(z), 0.0)
    b = root * ((0.5 * ti + 0.5) * xc)

    h = h_ref[...]
    for t in range(ts):
        h = a[t * nb:(t + 1) * nb, :] * h + b[t * nb:(t + 1) * nb, :]
        for c in range(n_lt):
            hs_ref[c, pl.ds(t, nb, stride=ts), :] = lanes(h, c)
    h_ref[...] = h
    h_bm = jnp.concatenate([hs_ref[c] for c in range(n_lt)], axis=1)
    o_ref[...] = (h_bm.reshape(nb, ts, w) * gy_ref[...].astype(F32)).astype(BF16)


def _rglru(xr, gy, cw, cb, wax, bax, lam):
    B, S, W = xr.shape
    ts = TS_LRU
    assert S % ts == 0
    tok = pl.BlockSpec((B, ts, W), lambda s: (0, s, 0))
    r_tm = jnp.arange(B * ts)
    perm = (r_tm[:, None] % B * ts + r_tm[:, None] // B == jnp.arange(B * ts)[None, :]).astype(BF16)
    return pl.pallas_call(
        _rglru_kernel,
        grid=(S // ts,),
        in_specs=[tok, tok, _const_spec(perm.shape), _const_spec(cw.shape), _const_spec(cb.shape),
                  _const_spec(wax.shape), _const_spec(bax.shape), _const_spec(lam.shape)],
        out_specs=tok,
        out_shape=jax.ShapeDtypeStruct((B, S, W), BF16),
        scratch_shapes=[pltpu.VMEM((W // LANES, B * ts, LANES), F32),
                        pltpu.VMEM(((CONV_W - 1) * B, W), F32), pltpu.VMEM((B, W), F32)],
        compiler_params=pltpu.CompilerParams(
            dimension_semantics=("arbitrary",), vmem_limit_bytes=VMEM_LIMIT_BYTES),
        name="rglru",
    )(xr, gy, perm, cw, cb, wax, bax, lam)


_NT = (((1,), (1,)), ((), ()))


def _select_bias(imp, cur, n_sel):
    n_slc, nqb = imp.shape
    blk = lax.broadcasted_iota(jnp.int32, (n_slc, 1), 0)
    causal = blk <= cur
    forced = (blk == 0) | (blk == cur) | (blk == cur - 1)
    score = jnp.where(causal, jnp.where(forced, FORCE_SCORE, imp), NEG)
    n_ch = n_slc // SUBLANES
    chunks = [score[c * SUBLANES:(c + 1) * SUBLANES] for c in range(n_ch)]
    ranks = [jnp.zeros((SUBLANES, nqb), F32) for _ in range(n_ch)]
    sub = lax.broadcasted_iota(jnp.int32, (SUBLANES, 1), 0)
    for k in range(n_slc):
        sk = jnp.broadcast_to(score[k:k + 1, :], (SUBLANES, nqb))
        k_chunk, k_row = divmod(k, SUBLANES)
        for c in range(n_ch):
            if c > k_chunk:
                one = jnp.where(sk >= chunks[c], 1.0, 0.0)
            elif c < k_chunk:
                one = jnp.where(sk > chunks[c], 1.0, 0.0)
            else:
                one = jnp.where(sub > k_row, jnp.where(sk >= chunks[c], 1.0, 0.0),
                                jnp.where(sk > chunks[c], 1.0, 0.0))
            ranks[c] = ranks[c] + one
    rank = jnp.concatenate(ranks, axis=0)
    return jnp.where(causal & (rank < float(n_sel)), 0.0, NEG)


def _nsa_kernel(q_ref, kc_ref, vo_ref, ks_ref, vst_ref, kw_ref, vwt_ref, g_ref, o_ref, bp_ref, qa_ref, s_ref):
    i = pl.program_id(1)
    n_grp = q_ref.shape[1]
    nq = HPG * Q_BLOCK
    lane = lax.broadcasted_iota(jnp.int32, (1, nq), 1)
    tl = lane & (Q_BLOCK - 1)
    t = i * Q_BLOCK + tl
    tq = i * Q_BLOCK + lax.broadcasted_iota(jnp.int32, (1, Q_BLOCK), 1)
    cur = lax.shift_right_logical(tq, SLC_BLOCK.bit_length() - 1)
    n_cmp_pad = kc_ref.shape[2]
    n_slc = vo_ref.shape[2] - HEAD_DIM - ONES_ROWS
    cmp_end = lax.broadcasted_iota(jnp.int32, (n_cmp_pad, 1), 0) * CMP_STRIDE + (CMP_BLOCK - 1)
    lane_q = lax.broadcasted_iota(jnp.int32, (nq, LANES), 1)
    row_q = lax.broadcasted_iota(jnp.int32, (Q_BLOCK, 1), 0)
    wk = WINDOW + Q_BLOCK
    w_off = pl.multiple_of(i * Q_BLOCK, Q_BLOCK)

    groups = range(n_grp)
    qs = [q_ref[0, g, 0] for g in groups]
    dot_nt = lambda a, b: lax.dot_general(a, b, _NT, preferred_element_type=F32)
    dot_nn = lambda a, b: jnp.dot(a, b, preferred_element_type=F32)


    sc = [dot_nt(kc_ref[0, g], qs[g]) for g in groups]
    sw = [dot_nt(kw_ref[0, g, pl.ds(w_off, wk), :], qs[g]) for g in groups]
    pc = []
    for g in groups:
        s = jnp.where(cmp_end <= t, sc[g], NEG)
        pc.append(jnp.exp2(s - jnp.max(s, axis=0, keepdims=True)).astype(BF16))
    n_val = HEAD_DIM + n_slc
    rc = []
    for g in groups:
        r = dot_nn(vo_ref[0, g], pc[g])
        r = r[:n_val] / r[n_val:n_val + 1]
        rc.append(jnp.where(t >= CMP_BLOCK - 1, r, 0.0))
    pw = []
    for g in groups:
        s = sw[g]
        s = jnp.concatenate([jnp.where(row_q > tl, s[:Q_BLOCK], NEG), s[Q_BLOCK:WINDOW],
                             jnp.where(row_q <= tl, s[WINDOW:], NEG)], axis=0)
        pw.append(jnp.exp2(s - jnp.max(s, axis=0, keepdims=True)).astype(BF16))
    o_win = []
    for g in groups:
        r = dot_nn(vwt_ref[0, g, :, pl.ds(w_off, wk)], pw[g])
        o_win.append(r[:HEAD_DIM] / r[HEAD_DIM:HEAD_DIM + 1])

    for g in groups:
        imp_h = rc[g][HEAD_DIM:]
        imp = imp_h[:, 0:Q_BLOCK]
        for h in range(1, HPG):
            imp = imp + imp_h[:, h * Q_BLOCK:(h + 1) * Q_BLOCK]
        bp_ref[g] = jnp.zeros(bp_ref.shape[1:], F32)
        bp_ref[g, HEAD_DIM:HEAD_DIM + n_slc, :] = _select_bias(imp, cur, min(N_SLC, n_slc))
        bt = bp_ref[g].T.astype(BF16)
        qa_ref[g] = jnp.where(lane_q < HEAD_DIM, qs[g], jnp.concatenate([bt] * HPG, axis=0))

    n_full = lax.shift_right_logical(i * Q_BLOCK, TK_SLC.bit_length() - 1)

    def key_off(u):
        return pl.multiple_of(jnp.where(u == 0, n_full, u - 1) * TK_SLC, TK_SLC)

    def scores(u, slot, masked):
        off = key_off(u)
        for g in groups:
            s = dot_nt(ks_ref[0, g, pl.ds(off, TK_SLC), :], qa_ref[g])
            if masked:
                s = jnp.where(off + lax.broadcasted_iota(jnp.int32, (TK_SLC, 1), 0) <= t, s, NEG)
            s_ref[slot, g] = s

    def consume(u, slot, carry):
        off = key_off(u)
        ps, stats = [], []
        for g in groups:
            m, acc = carry[g]
            s = s_ref[slot, g]
            m_new = jnp.maximum(m, jnp.max(s, axis=0, keepdims=True))
            stats.append((m_new, jnp.exp2(m - m_new) * acc))
            ps.append(jnp.exp2(s - m_new).astype(BF16))
        pv = [dot_nn(vst_ref[0, g, :, pl.ds(off, TK_SLC)], ps[g]) for g in groups]
        return tuple((stats[g][0], stats[g][1] + pv[g]) for g in groups)

    def pipeline(first, a, b, carry):
        n_pairs = lax.shift_right_logical(n_full + 1 - first, 1)

        def pair_step(j, c):
            u = first + 2 * j
            scores(u + 1, b, False)
            c = consume(u, a, c)
            scores(u + 2, a, False)
            return consume(u + 1, b, c)

        def drain(c):
            scores(n_full, b, False)
            return consume(n_full, b, consume(n_full - 1, a, c))

        carry = lax.fori_loop(0, n_pairs - 1, pair_step, carry)
        return lax.cond(n_pairs > 0, drain, lambda c: c, carry)

    def odd_count(carry):
        scores(1, 1, False)
        return pipeline(1, 1, 0, consume(0, 0, carry))

    init = (jnp.full((1, nq), NEG, F32), jnp.zeros((vst_ref.shape[2], nq), F32))
    scores(0, 0, True)
    carry = lax.cond((n_full & 1) == 0, odd_count, functools.partial(pipeline, 0, 0, 1), (init,) * n_grp)

    gt = g_ref[0]
    for g in groups:
        _, acc_s = carry[g]
        o_slc = acc_s[:HEAD_DIM] / acc_s[HEAD_DIM:HEAD_DIM + 1]
        branches = (rc[g][:HEAD_DIM], o_slc, o_win[g])
        heads = []
        for h in range(HPG):
            cols = slice(h * Q_BLOCK, (h + 1) * Q_BLOCK)
            o_h = 0.0
            for br in range(N_BRANCH):
                r = (br * n_grp + g) * HPG + h
                o_h = o_h + gt[r:r + 1, :] * branches[br][:, cols]
            heads.append(o_h)
        for pr in range(HPG // 2):
            pair = jnp.concatenate(heads[2 * pr:2 * pr + 2], axis=0)
            c0 = (g * HPG + 2 * pr) * HEAD_DIM
            o_ref[0, :, c0:c0 + 2 * HEAD_DIM] = pair.T.astype(BF16)


def _nsa(q, kc, vo, ks, vst, kw, vwt, g):
    B, G, nqb, nq, _ = q.shape
    S = ks.shape[2]
    n_slc = S // SLC_BLOCK
    assert HEAD_DIM + n_slc <= LANES and S % TK_SLC == 0
    per_q = lambda shape: pl.BlockSpec((1, G, 1) + shape, lambda b, i: (b, 0, i, 0, 0))
    per_b = lambda a: pl.BlockSpec((1,) + a.shape[1:], lambda b, i: (b, 0, 0, 0))
    return pl.pallas_call(
        _nsa_kernel,
        grid=(B, nqb),
        in_specs=[per_q((nq, LANES)), per_b(kc), per_b(vo), per_b(ks), per_b(vst), per_b(kw), per_b(vwt),
                  pl.BlockSpec((1, g.shape[1], Q_BLOCK), lambda b, i: (b, 0, i))],
        out_specs=pl.BlockSpec((1, Q_BLOCK, NSA_WIDTH), lambda b, i: (b, i, 0)),
        out_shape=jax.ShapeDtypeStruct((B, S, NSA_WIDTH), BF16),
        scratch_shapes=[pltpu.VMEM((G, LANES, Q_BLOCK), F32), pltpu.VMEM((G, nq, LANES), BF16),
                        pltpu.VMEM((2, G, TK_SLC, nq), F32)],
        compiler_params=pltpu.CompilerParams(
            dimension_semantics=("parallel", "arbitrary"), vmem_limit_bytes=VMEM_LIMIT_BYTES),
        name="nsa",
    )(q, kc, vo, ks, vst, kw, vwt, g)


def _out_kernel(x_ref, on_ref, ol_ref, gm_ref, wa_ref, wb_ref, wo_ref, n2_ref, w1_ref, w2_ref, o_ref):
    ua = jnp.dot(on_ref[...], wa_ref[...], preferred_element_type=F32)
    ub = jnp.dot(ol_ref[...], wb_ref[...], preferred_element_type=F32)
    gm = gm_ref[...]
    merged = gm[:, :D_MODEL].astype(F32) * ua + gm[:, D_MODEL:].astype(F32) * ub
    h = x_ref[...] + jnp.dot(merged.astype(BF16), wo_ref[...], preferred_element_type=F32)
    ms = jnp.mean(h * h, axis=-1, keepdims=True)
    hn = (h * lax.rsqrt(ms + EPS) * n2_ref[...]).astype(BF16)
    acc = h
    for c in range(D_FF // FF_CHUNK):
        u = jnp.dot(hn, w1_ref[:, c * FF_CHUNK:(c + 1) * FF_CHUNK], preferred_element_type=F32)
        u = jnp.square(jnp.maximum(u, 0.0)).astype(BF16)
        acc = acc + jnp.dot(u, w2_ref[c * FF_CHUNK:(c + 1) * FF_CHUNK, :], preferred_element_type=F32)
    o_ref[...] = acc


def _out(x2, on, ol, gm, wa, wb, wo, n2, w1, w2):
    T, D = x2.shape
    tm = TM_OUT
    tok = lambda width: pl.BlockSpec((tm, width), lambda i: (i, 0))
    return pl.pallas_call(
        _out_kernel,
        grid=(T // tm,),
        in_specs=[tok(D), tok(NSA_WIDTH), tok(LRU_WIDTH), tok(2 * D_MODEL),
                  _const_spec(wa.shape), _const_spec(wb.shape), _const_spec(wo.shape), _const_spec(n2.shape),
                  _const_spec(w1.shape), _const_spec(w2.shape)],
        out_specs=tok(D),
        out_shape=jax.ShapeDtypeStruct((T, D), F32),
        compiler_params=pltpu.CompilerParams(
            dimension_semantics=("parallel",), vmem_limit_bytes=VMEM_LIMIT_BYTES),
        name="out",
    )(x2, on, ol, gm, wa, wb, wo, n2, w1, w2)


def _rope_tables(pos):
    inv = ROPE_THETA ** (-jnp.arange(ROPE_HALF, dtype=F32) / ROPE_HALF)
    ang = pos.astype(F32)[:, None] * inv[None, :]
    cos, sin = jnp.cos(ang), jnp.sin(ang)
    n = pos.shape[0]
    rest = HEAD_DIM - ROPE_DIM
    c = jnp.concatenate([cos, cos, jnp.ones((n, rest), F32)], axis=1)
    sa = jnp.concatenate([-sin, jnp.zeros((n, HEAD_DIM - ROPE_HALF), F32)], axis=1)
    sb = jnp.concatenate([jnp.zeros((n, ROPE_HALF), F32), sin, jnp.zeros((n, rest), F32)], axis=1)
    two = lambda a: jnp.concatenate([a] * (LANES // HEAD_DIM), axis=1)
    return two(c), two(sa), two(sb)


def _layer(h, norm1_w, w_in, q_norm_w, k_norm_w, phi_k_pos, phi_k_w1, phi_k_w2, phi_v_pos, phi_v_w1, phi_v_w2,
           conv_w, conv_b, lru_wa, lru_ba, lru_wx, lru_bx, lru_lambda, w_nsa_up, w_lru_up, w_o, norm2_w,
           w_ff1, w_ff2):
    B, S, D = h.shape
    assert D == D_MODEL and S % TK_SLC == 0 and S % TM_IN == 0 and (B * S) % TM_OUT == 0
    nqb = S // Q_BLOCK
    n_chunks = S // CMP_STRIDE
    n_cmp = (S - CMP_BLOCK) // CMP_STRIDE + 1
    n_slc = S // SLC_BLOCK

    w_pad = jnp.concatenate(
        [w_in[:, :G_REAL_END], jnp.zeros((D, G_PAD - N_BRANCH * N_HEADS), w_in.dtype), w_in[:, G_REAL_END:]],
        axis=1)
    col = jnp.arange(IN_COLS_PAD)
    is_gate = ((col >= OFF_G) & (col < OFF_XR)) | (col >= OFF_GM)
    w_pad = (w_pad * jnp.where(is_gate, 0.5, 1.0)).astype(BF16)
    w_t = jnp.concatenate([w_pad[:, OFF_VS:OFF_VS + KV_WIDTH], w_pad[:, OFF_VW:OFF_VW + KV_WIDTH],
                           w_pad[:, OFF_G:OFF_G + GATE_ROWS]], axis=1).T
    qnw = jnp.tile(q_norm_w, N_HEADS)[None, :]
    ksw = jnp.tile(k_norm_w[1], N_KV)[None, :]
    kww = jnp.tile(k_norm_w[2], N_KV)[None, :]
    rc, ra, rb = _rope_tables(jnp.arange(S))
    seg = jnp.arange(NSA_WIDTH) // HEAD_DIM
    bd = (seg[:, None] == seg[None, :]).astype(BF16)

    q5, kcv, ks_aug, vst_aug, kw_aug, vwt_aug, gt, xr, gy, gm = _in_proj(
        h, norm1_w[None, :], w_pad, w_t, qnw, ksw, kww, rc, ra, rb, bd)

    def chunks(raw):
        return raw.reshape(B, S, N_KV, HEAD_DIM).transpose(0, 2, 1, 3).reshape(
            B, N_KV, n_chunks, CMP_STRIDE * HEAD_DIM)

    pos2 = lambda p: p.reshape(2, CMP_STRIDE * HEAD_DIM)
    pad_w2 = lambda w: jnp.concatenate([w, jnp.zeros((PHI_HIDDEN, LANES - HEAD_DIM), w.dtype)], axis=1).astype(BF16)
    knw0 = jnp.concatenate([k_norm_w[0], jnp.zeros((LANES - HEAD_DIM,), F32)])[None, :]
    cmp_end = jnp.arange(n_chunks) * CMP_STRIDE + (CMP_BLOCK - 1)
    cc, ca, cb_ = _rope_tables(cmp_end)
    kc, vc = _compress(chunks(kcv[..., :KV_WIDTH]), chunks(kcv[..., KV_WIDTH:]),
                       pos2(phi_k_pos), pos2(phi_v_pos), phi_k_w1.astype(BF16), phi_v_w1.astype(BF16),
                       pad_w2(phi_k_w2), pad_w2(phi_v_w2), knw0, cc, ca, cb_)
    cmp_lo = jnp.arange(n_chunks) * CMP_STRIDE
    slc_lo = jnp.arange(n_slc) * SLC_BLOCK
    overlap_t = ((cmp_lo[None, :] <= slc_lo[:, None] + SLC_BLOCK - 1)
                 & (cmp_lo[None, :] + CMP_BLOCK - 1 >= slc_lo[:, None])
                 & (jnp.arange(n_chunks)[None, :] < n_cmp)).astype(BF16)
    ones_rows = lambda n: jnp.ones((B, N_KV, ONES_ROWS, n), BF16)
    vo = jnp.concatenate(
        [vc[..., :HEAD_DIM].transpose(0, 1, 3, 2), jnp.broadcast_to(overlap_t, (B, N_KV) + overlap_t.shape),
         ones_rows(n_chunks)], axis=2)

    wax = (0.5 * jnp.concatenate([lru_wa, lru_wx], axis=2)).astype(BF16)
    bax = 0.5 * jnp.stack([lru_ba.reshape(LRU_WIDTH), lru_bx.reshape(LRU_WIDTH)])
    o_lru = _rglru(xr, gy, conv_w, conv_b[None, :], wax, bax, lru_lambda[None, :])

    o_nsa = _nsa(q5, kc, vo, ks_aug, vst_aug, kw_aug, vwt_aug, gt).reshape(B * S, NSA_WIDTH)

    out = _out(h.reshape(B * S, D), o_nsa, o_lru.reshape(B * S, LRU_WIDTH), gm.reshape(B * S, 2 * D_MODEL),
               w_nsa_up.astype(BF16), w_lru_up.astype(BF16), w_o.astype(BF16), norm2_w[None, :],
               w_ff1.astype(BF16), w_ff2.astype(BF16))
    return out.reshape(B, S, D)


def kernel(x, norm1_w, w_in, q_norm_w, k_norm_w, phi_k_pos, phi_k_w1, phi_k_w2, phi_v_pos, phi_v_w1, phi_v_w2, conv_w, conv_b, lru_wa, lru_ba, lru_wx, lru_bx, lru_lambda, w_nsa_up, w_lru_up, w_o, norm2_w, w_ff1, w_ff2):
    params = (norm1_w, w_in, q_norm_w, k_norm_w, phi_k_pos, phi_k_w1, phi_k_w2, phi_v_pos, phi_v_w1, phi_v_w2,
              conv_w, conv_b, lru_wa, lru_ba, lru_wx, lru_bx, lru_lambda, w_nsa_up, w_lru_up, w_o, norm2_w,
              w_ff1, w_ff2)
    h = x
    for layer in range(w_in.shape[0]):
        h = _layer(h, *(p[layer] for p in params))
    return h
```
